```python
import math
import jax, jax.numpy as jnp
from jax import lax
import numpy as np

D_MODEL = 2048
BATCH = 4
SEQ = 2048
DEPTH = 2

HEAD_DIM = 64
N_MIXERS = 4
GROUP_WIDTH = D_MODEL // N_MIXERS
N_FOX_HEADS = GROUP_WIDTH // HEAD_DIM
N_SB_HEADS = GROUP_WIDTH // HEAD_DIM
N_DIFF_HEADS = GROUP_WIDTH // (2 * HEAD_DIM)
N_DIL_HEADS = GROUP_WIDTH // HEAD_DIM
MIX_WIDTH = N_MIXERS * GROUP_WIDTH
D_FF = 4 * D_MODEL
QUERY_BLOCK = 128
DILATED_BRANCHES = ((128, 1), (512, 4), (2048, 16))
ALIBI_MAX_EXP = 8.0
LN_EPS = 1e-5
RMS_EPS = 1e-5
FORGET_BIAS_INIT = 2.0
DEEPNORM_ALPHA = (2 * DEPTH) ** 0.25
DEEPNORM_BETA = (8 * DEPTH) ** -0.25
IN_SPLITS = (GROUP_WIDTH,) * 3 + (N_FOX_HEADS,) + (GROUP_WIDTH,) * 9
IN_WIDTH = 12 * GROUP_WIDTH + N_FOX_HEADS
V_SEGMENTS = (2, 6, 9, 12)

kernel_name = "hybrid_fox_stickbreak_diff_dilated_deepnorm"


def _layer_norm(x, g, b):
    xf = x.astype(jnp.float32)
    mu = jnp.mean(xf, axis=-1, keepdims=True)
    var = jnp.mean(jnp.square(xf - mu), axis=-1, keepdims=True)
    return ((xf - mu) * lax.rsqrt(var + LN_EPS) * g + b).astype(x.dtype)


def _alibi_slopes():
    n = N_DIFF_HEADS + N_DIL_HEADS
    return jnp.exp2(-ALIBI_MAX_EXP * jnp.arange(1, n + 1, dtype=jnp.float32) / n)


def _heads(a, n):
    B, S, _ = a.shape
    return a.reshape(B, S, n, -1).transpose(0, 2, 1, 3)


def _merge_heads(o):
    B, H, S, d = o.shape
    return o.transpose(0, 2, 1, 3).reshape(B, S, H * d)


def _to_query_blocks(a):
    B, H, S = a.shape[:3]
    a = a.reshape(B, H, S // QUERY_BLOCK, QUERY_BLOCK, *a.shape[3:])
    return jnp.moveaxis(a, 2, 0)


def _from_query_blocks(o):
    nb, B, H, qb = o.shape[:4]
    return jnp.moveaxis(o, 0, 2).reshape(B, H, nb * qb, *o.shape[4:])


def _query_positions(S):
    return jnp.arange(S).reshape(S // QUERY_BLOCK, QUERY_BLOCK)


def _forgetting_attention(q, k, v, log_f_cum):
    S = q.shape[2]
    scale = q.shape[-1] ** -0.5
    kpos = jnp.arange(S)

    def block(args):
        qb, cb, tq = args
        s = jnp.einsum('bhqd,bhkd->bhqk', qb, k).astype(jnp.float32) * scale
        s = s + cb[..., None] - log_f_cum[:, :, None, :]
        s = jnp.where(kpos[None, :] <= tq[:, None], s, -jnp.inf)
        p = jax.nn.softmax(s, axis=-1)
        return jnp.einsum('bhqk,bhkd->bhqd', p.astype(v.dtype), v)

    out = lax.map(block, (_to_query_blocks(q), _to_query_blocks(log_f_cum), _query_positions(S)))
    return _from_query_blocks(out)


def _stick_breaking_attention(q, k, v):
    S = q.shape[2]
    scale = q.shape[-1] ** -0.5
    kpos = jnp.arange(S)

    def block(args):
        qb, tq = args
        z = jnp.einsum('bhqd,bhkd->bhqk', qb, k).astype(jnp.float32) * scale
        strict = kpos[None, :] < tq[:, None]
        log_beta = jax.nn.log_sigmoid(z)
        log_one_minus = jnp.where(strict, jax.nn.log_sigmoid(-z), 0.0)
        later = lax.cumsum(log_one_minus, axis=3, reverse=True) - log_one_minus
        a = jnp.where(strict, jnp.exp(log_beta + later), 0.0)
        return jnp.einsum('bhqk,bhkd->bhqd', a.astype(v.dtype), v)

    out = lax.map(block, (_to_query_blocks(q), _query_positions(S)))
    return _from_query_blocks(out)


def _differential_attention(q1, q2, k1, k2, v, lam, slopes):
    S = q1.shape[2]
    scale = q1.shape[-1] ** -0.5
    kpos = jnp.arange(S)

    def block(args):
        q1b, q2b, tq = args
        dist = tq[:, None] - kpos[None, :]
        causal = dist >= 0
        bias = -slopes[:, None, None] * dist.astype(jnp.float32)

        def probs(qb, kk):
            s = jnp.einsum('bhqd,bhkd->bhqk', qb, kk).astype(jnp.float32) * scale + bias
            return jax.nn.softmax(jnp.where(causal, s, -jnp.inf), axis=-1)

        a = probs(q1b, k1) - lam * probs(q2b, k2)
        return jnp.einsum('bhqk,bhkd->bhqd', a.astype(v.dtype), v)

    out = lax.map(block, (_to_query_blocks(q1), _to_query_blocks(q2), _query_positions(S)))
    return _from_query_blocks(out)


def _dilated_branch(q, k, v, slopes, window, dilation):
    B, S, H, d = q.shape
    n = window // dilation
    unit = n * dilation
    Lp = -(-S // unit) * unit
    nb = Lp // unit

    def to_blocks(a):
        a = jnp.pad(a, ((0, 0), (0, Lp - S), (0, 0), (0, 0)))
        a = a.reshape(B, Lp // dilation, dilation, H, d).transpose(0, 2, 3, 1, 4)
        return a.reshape(B, dilation, H, nb, n, d)

    def with_prev(a):
        prev = jnp.concatenate([jnp.zeros_like(a[:, :, :, :1]), a[:, :, :, :-1]], axis=3)
        return jnp.concatenate([prev, a], axis=4)

    qb = to_blocks(q)
    kk = with_prev(to_blocks(k))
    vv = with_prev(to_blocks(v))
    s = jnp.einsum('brhnqd,brhnkd->brhnqk', qb, kk).astype(jnp.float32) * d ** -0.5
    i = jnp.arange(n)[:, None]
    j = jnp.arange(2 * n)[None, :]
    steps = n + i - j
    in_band = (steps >= 0) & (steps <= n)
    has_key = (jnp.arange(nb)[:, None, None] > 0) | (j[None] >= n)
    valid = in_band[None] & has_key
    s = s - slopes[:, None, None, None] * (steps * dilation).astype(jnp.float32)
    s = jnp.where(valid, s, -jnp.inf)
    m = jnp.max(s, axis=-1, keepdims=True)
    e = jnp.exp(s - m)
    denom = jnp.sum(e, axis=-1, keepdims=True)
    o = jnp.einsum('brhnqk,brhnkd->brhnqd', (e / denom).astype(v.dtype), vv)
    lse = (m + jnp.log(denom))[..., 0]

    def from_blocks(a):
        rest = a.shape[5:]
        a = a.reshape(B, dilation, H, Lp // dilation, *rest)
        perm = (0, 3, 1, 2) + tuple(range(4, a.ndim))
        return a.transpose(perm).reshape(B, Lp, H, *rest)[:, :S]

    return from_blocks(o), from_blocks(lse)


def _dilated_attention(q, k, v, slopes):
    outs, lses = [], []
    for window, dilation in DILATED_BRANCHES:
        o, lse = _dilated_branch(q, k, v, slopes, window, dilation)
        outs.append(o)
        lses.append(lse)
    w = jax.nn.softmax(jnp.stack(lses), axis=0)
    o = jnp.sum(w[..., None] * jnp.stack(outs).astype(jnp.float32), axis=0)
    return o.astype(v.dtype)


def _hybrid_layer(x, layer_idx, w_in, b_f, lq1, lk1, lq2, lk2, subln_g, w_out,
                  ln1_g, ln1_b, w1, w2, ln2_g, ln2_b):
    B, S, _ = x.shape
    h = jnp.einsum('bsd,de->bse', x, w_in)
    parts = jnp.split(h, np.cumsum(IN_SPLITS)[:-1].tolist(), axis=-1)
    fq, fk, fv, fz, sq, sk, sv, dq, dk, dv, gq, gk, gv = parts
    slopes = _alibi_slopes()

    log_f = jax.nn.log_sigmoid(fz.astype(jnp.float32) + b_f.astype(jnp.float32))
    log_f_cum = jnp.cumsum(log_f, axis=1).transpose(0, 2, 1)
    o_fox = _forgetting_attention(_heads(fq, N_FOX_HEADS), _heads(fk, N_FOX_HEADS),
                                  _heads(fv, N_FOX_HEADS), log_f_cum)

    o_sb = _stick_breaking_attention(_heads(sq, N_SB_HEADS), _heads(sk, N_SB_HEADS),
                                     _heads(sv, N_SB_HEADS))

    dq = dq.reshape(B, S, N_DIFF_HEADS, 2, HEAD_DIM).transpose(0, 2, 3, 1, 4)
    dk = dk.reshape(B, S, N_DIFF_HEADS, 2, HEAD_DIM).transpose(0, 2, 3, 1, 4)
    lam_init = 0.8 - 0.6 * math.exp(-0.3 * layer_idx)
    lam = (jnp.exp(jnp.sum(lq1.astype(jnp.float32) * lk1.astype(jnp.float32)))
           - jnp.exp(jnp.sum(lq2.astype(jnp.float32) * lk2.astype(jnp.float32))) + lam_init)
    od = _differential_attention(dq[:, :, 0], dq[:, :, 1], dk[:, :, 0], dk[:, :, 1],
                                 _heads(dv, N_DIFF_HEADS), lam, slopes[:N_DIFF_HEADS])
    odf = od.astype(jnp.float32)
    odf = odf * lax.rsqrt(jnp.mean(jnp.square(odf), axis=-1, keepdims=True) + RMS_EPS)
    o_diff = (odf * subln_g * (1.0 - lam_init)).astype(x.dtype)

    o_dil = _dilated_attention(gq.reshape(B, S, N_DIL_HEADS, HEAD_DIM),
                               gk.reshape(B, S, N_DIL_HEADS, HEAD_DIM),
                               gv.reshape(B, S, N_DIL_HEADS, HEAD_DIM),
                               slopes[N_DIFF_HEADS:])

    mixed = jnp.concatenate([_merge_heads(o_fox), _merge_heads(o_sb), _merge_heads(o_diff),
                             o_dil.reshape(B, S, -1)], axis=-1)
    y = jnp.einsum('bse,ed->bsd', mixed, w_out)
    x = _layer_norm(DEEPNORM_ALPHA * x + y, ln1_g, ln1_b)

    a = jax.nn.relu(jnp.einsum('bsd,df->bsf', x, w1))
    y = jnp.einsum('bsf,fd->bsd', a * a, w2)
    return _layer_norm(DEEPNORM_ALPHA * x + y, ln2_g, ln2_b)


def setup_inputs(seed: int = 0) -> dict:
    key = jax.random.key(seed)
    ks = jax.random.split(key, 16)
    nrm = jax.random.normal
    col_scale = np.ones((IN_WIDTH,), np.float32)
    off = np.concatenate([[0], np.cumsum(IN_SPLITS)])
    for seg in V_SEGMENTS:
        col_scale[off[seg]:off[seg + 1]] = DEEPNORM_BETA
    x = nrm(ks[0], (BATCH, SEQ, D_MODEL), jnp.float32)
    w_in = nrm(ks[1], (DEPTH, D_MODEL, IN_WIDTH), jnp.float32) * (D_MODEL ** -0.5) * jnp.asarray(col_scale)
    fox_forget_bias = FORGET_BIAS_INIT + 0.1 * nrm(ks[2], (DEPTH, N_FOX_HEADS), jnp.float32)
    diff_lambda_q1 = 0.1 * nrm(ks[3], (DEPTH, HEAD_DIM), jnp.float32)
    diff_lambda_k1 = 0.1 * nrm(ks[4], (DEPTH, HEAD_DIM), jnp.float32)
    diff_lambda_q2 = 0.1 * nrm(ks[5], (DEPTH, HEAD_DIM), jnp.float32)
    diff_lambda_k2 = 0.1 * nrm(ks[6], (DEPTH, HEAD_DIM), jnp.float32)
    diff_subln_gain = 1.0 + 0.02 * nrm(ks[7], (DEPTH, 2 * HEAD_DIM), jnp.float32)
    w_out = nrm(ks[8], (DEPTH, MIX_WIDTH, D_MODEL), jnp.float32) * (MIX_WIDTH ** -0.5) * DEEPNORM_BETA
    ln1_gain = 1.0 + 0.02 * nrm(ks[9], (DEPTH, D_MODEL), jnp.float32)
    ln1_bias = 0.02 * nrm(ks[10], (DEPTH, D_MODEL), jnp.float32)
    w_mlp_in = nrm(ks[11], (DEPTH, D_MODEL, D_FF), jnp.float32) * (D_MODEL ** -0.5) * DEEPNORM_BETA
    w_mlp_out = nrm(ks[12], (DEPTH, D_FF, D_MODEL), jnp.float32) * (D_FF ** -0.5) * DEEPNORM_BETA
    ln2_gain = 1.0 + 0.02 * nrm(ks[13], (DEPTH, D_MODEL), jnp.float32)
    ln2_bias = 0.02 * nrm(ks[14], (DEPTH, D_MODEL), jnp.float32)
    return {"x": x, "w_in": w_in, "fox_forget_bias": fox_forget_bias,
            "diff_lambda_q1": diff_lambda_q1, "diff_lambda_k1": diff_lambda_k1,
            "diff_lambda_q2": diff_lambda_q2, "diff_lambda_k2": diff_lambda_k2,
            "diff_subln_gain": diff_subln_gain, "w_out": w_out,
            "ln1_gain": ln1_gain, "ln1_bias": ln1_bias,
            "w_mlp_in": w_mlp_in, "w_mlp_out": w_mlp_out,
            "ln2_gain": ln2_gain, "ln2_bias": ln2_bias}


def reference(x, w_in, fox_forget_bias, diff_lambda_q1, diff_lambda_k1, diff_lambda_q2,
              diff_lambda_k2, diff_subln_gain, w_out, ln1_gain, ln1_bias, w_mlp_in,
              w_mlp_out, ln2_gain, ln2_bias):
    for l in range(DEPTH):
        x = _hybrid_layer(x, l, w_in[l], fox_forget_bias[l], diff_lambda_q1[l], diff_lambda_k1[l],
                          diff_lambda_q2[l], diff_lambda_k2[l], diff_subln_gain[l], w_out[l],
                          ln1_gain[l], ln1_bias[l], w_mlp_in[l], w_mlp_out[l],
                          ln2_gain[l], ln2_bias[l])
    return x
```

```python
import functools
import math

import numpy as np
import jax
import jax.numpy as jnp
from jax import lax
from jax.experimental import pallas as pl
from jax.experimental.pallas import tpu as pltpu

D_MODEL = 2048
DEPTH = 2
HEAD_DIM = 64
GROUP_WIDTH = 512
N_FOX_HEADS = 8
N_DIFF_HEADS = 4
N_DIL_HEADS = 8
D_FF = 4 * D_MODEL
DILATED_BRANCHES = ((128, 1), (512, 4), (2048, 16))
ALIBI_MAX_EXP = 8.0
LN_EPS = 1e-5
RMS_EPS = 1e-5
DEEPNORM_ALPHA = (2 * DEPTH) ** 0.25
QK_SCALE = HEAD_DIM ** -0.5

LANES = 128
HEADS_PER_BLOCK = LANES // HEAD_DIM
VMEM_LIMIT_BYTES = 56 * 1024 * 1024
MASKED = -1e30

_SEG = {name: 4 * i for i, name in enumerate(
    ("fq", "fk", "fv", "sq", "sk", "sv", "dq", "dk", "dv", "gq", "gk", "gv"))}

ATT_TQ = 256
ATT_TK = 256

_F32 = jnp.float32
_BF16 = jnp.bfloat16


def _params(*sem):
    return pltpu.CompilerParams(dimension_semantics=sem, vmem_limit_bytes=VMEM_LIMIT_BYTES)


def _dot(a, b):
    return jnp.dot(a, b, preferred_element_type=_F32)


def _dot_nt(a, b):
    return lax.dot_general(a, b, (((1,), (1,)), ((), ())), preferred_element_type=_F32)


def _matmul_kernel(x_ref, w_ref, o_ref):
    o_ref[...] = _dot(x_ref[...], w_ref[...]).astype(o_ref.dtype)


def _in_projection(xb, w, tm=1024, tn=1024):
    m, k = xb.shape
    n = w.shape[1]
    return pl.pallas_call(
        _matmul_kernel,
        grid=(n // tn, m // tm),
        in_specs=[pl.BlockSpec((tm, k), lambda j, i: (i, 0)),
                  pl.BlockSpec((k, tn), lambda j, i: (0, j))],
        out_specs=pl.BlockSpec((tm, tn), lambda j, i: (i, j)),
        out_shape=jax.ShapeDtypeStruct((m, n), _BF16),
        compiler_params=_params("arbitrary", "arbitrary"),
    )(xb, w)


def _split3(x):
    hi = x.astype(_BF16)
    r = x - hi.astype(_F32)
    mid = r.astype(_BF16)
    lo = (r - mid.astype(_F32)).astype(_BF16)
    return hi, mid, lo


def _log_sigmoid_pair(z):
    lp = jnp.log1p(jnp.exp(-jnp.abs(z)))
    return jnp.minimum(z, 0.0) - lp, jnp.minimum(-z, 0.0) - lp


def _forget_kernel(x_ref, w_ref, b_ref, c_ref, ct_ref, *, blk):
    s = x_ref.shape[1]
    z = _dot(x_ref[0], w_ref[...]) + b_ref[...]
    log_f, _ = _log_sigmoid_pair(z)
    r = lax.broadcasted_iota(jnp.int32, (blk, blk), 0)
    c = lax.broadcasted_iota(jnp.int32, (blk, blk), 1)
    tri = (c <= r).astype(_BF16)
    carry = jnp.zeros((1, LANES), _F32)
    for i in range(s // blk):
        hi, mid, lo = _split3(log_f[i * blk:(i + 1) * blk])
        cum = _dot(tri, hi) + _dot(tri, mid) + _dot(tri, lo) + carry
        c_ref[0, i * blk:(i + 1) * blk, :] = cum
        carry = cum[blk - 1:blk, :]
    ct_ref[0] = c_ref[0].T


def _forget_cumsum(xb3, w_fz, b_fz, blk=256):
    b, s, d = xb3.shape
    return pl.pallas_call(
        functools.partial(_forget_kernel, blk=blk),
        grid=(b,),
        in_specs=[pl.BlockSpec((1, s, d), lambda i: (i, 0, 0)),
                  pl.BlockSpec((d, LANES), lambda i: (0, 0)),
                  pl.BlockSpec((1, LANES), lambda i: (0, 0))],
        out_specs=[pl.BlockSpec((1, s, LANES), lambda i: (i, 0, 0)),
                   pl.BlockSpec((1, LANES, s), lambda i: (i, 0, 0))],
        out_shape=[jax.ShapeDtypeStruct((b, s, LANES), _F32),
                   jax.ShapeDtypeStruct((b, LANES, s), _F32)],
        compiler_params=_params("arbitrary"),
    )(xb3, w_fz, b_fz)


def _causal_iota(t):
    r = lax.broadcasted_iota(jnp.int32, (t, t), 0)
    c = lax.broadcasted_iota(jnp.int32, (t, t), 1)
    return r, c


def _softmax_step(carry, s, v):
    m, l, acc = carry
    m_new = jnp.maximum(m, jnp.max(s, axis=-1, keepdims=True))
    p = jnp.exp(s - m_new)
    alpha = jnp.exp(m - m_new)
    l = alpha * l + jnp.sum(p, axis=-1, keepdims=True)
    acc = alpha * acc + _dot(p.astype(_BF16), v)
    return m_new, l, acc


def _softmax_init(tq, dv):
    return (jnp.full((tq, 1), MASKED, _F32), jnp.zeros((tq, 1), _F32), jnp.zeros((tq, dv), _F32))


def _fox_kernel(q_ref, k_ref, v_ref, c_ref, ct_ref, o_ref):
    t = q_ref.shape[1]
    hp, qi = pl.program_id(1), pl.program_id(2)
    row, col = _causal_iota(t)
    causal = row >= col
    lane = lax.broadcasted_iota(jnp.int32, (t, LANES), 1)
    outs = []
    for i in range(HEADS_PER_BLOCK):
        cols = slice(i * HEAD_DIM, (i + 1) * HEAD_DIM)
        q = q_ref[0, :, cols] * QK_SCALE
        cq = jnp.sum(jnp.where(lane == HEADS_PER_BLOCK * hp + i, c_ref[0], 0.0), axis=-1, keepdims=True)

        def step(j, carry, diagonal, q=q, cq=cq, cols=cols, i=i):
            rows = pl.ds(pl.multiple_of(j * t, t), t)
            s = _dot_nt(q, k_ref[0, rows, cols]) + cq - ct_ref[0, i, j]
            if diagonal:
                s = jnp.where(causal, s, MASKED)
            return _softmax_step(carry, s, v_ref[0, rows, cols])

        carry = lax.fori_loop(0, qi, functools.partial(step, diagonal=False), _softmax_init(t, HEAD_DIM))
        _, l, acc = step(qi, carry, True)
        outs.append(acc / l)
    o_ref[0] = jnp.concatenate(outs, axis=-1).astype(o_ref.dtype)


def _sb_kernel(q_ref, k_ref, v_ref, o_ref):
    t = q_ref.shape[1]
    qi = pl.program_id(2)
    row, col = _causal_iota(t)
    strict = row > col
    later_ones = strict.astype(_BF16)
    outs = []
    for i in range(HEADS_PER_BLOCK):
        cols = slice(i * HEAD_DIM, (i + 1) * HEAD_DIM)
        q = q_ref[0, :, cols] * QK_SCALE

        def step(j, carry, diagonal, q=q, cols=cols):
            tail, acc = carry
            rows = pl.ds(pl.multiple_of(j * t, t), t)
            z = _dot_nt(q, k_ref[0, rows, cols])
            log_beta, log_rest = _log_sigmoid_pair(z)
            if diagonal:
                log_rest = jnp.where(strict, log_rest, 0.0)
            hi = log_rest.astype(_BF16)
            lo = (log_rest - hi.astype(_F32)).astype(_BF16)
            later = _dot(hi, later_ones) + _dot(lo, later_ones)
            a = jnp.exp(log_beta + later + tail)
            if diagonal:
                a = jnp.where(strict, a, 0.0)
            acc = acc + _dot(a.astype(_BF16), v_ref[0, rows, cols])
            tail = tail + jnp.sum(log_rest, axis=-1, keepdims=True)
            return tail, acc

        carry = step(qi, (jnp.zeros((t, 1), _F32), jnp.zeros((t, HEAD_DIM), _F32)), True)
        _, acc = lax.fori_loop(
            0, qi, lambda n, c, step=step: step(qi - 1 - n, c, False), carry)
        outs.append(acc)
    o_ref[0] = jnp.concatenate(outs, axis=-1).astype(o_ref.dtype)


def _alibi_step(carry, s, v, offset):
    m, l, acc = carry
    m_new = jnp.maximum(m, jnp.max(s, axis=-1, keepdims=True) - offset)
    p = jnp.exp(s - (m_new + offset))
    alpha = jnp.exp(m - m_new)
    l = alpha * l + jnp.sum(p, axis=-1, keepdims=True)
    acc = alpha * acc + _dot(p.astype(_BF16), v)
    return m_new, l, acc


def _diff_kernel(slopes_ref, q_ref, k_ref, v_ref, lam_ref, g_ref, o_ref, *, lam_init):
    t = q_ref.shape[1]
    h, qi = pl.program_id(1), pl.program_id(2)
    slope = slopes_ref[h]
    row, col = _causal_iota(t)
    causal = row >= col
    key_bias = slope * lax.broadcasted_iota(jnp.int32, (1, t), 1).astype(_F32)
    lq1, lk1, lq2, lk2 = (lam_ref[n:n + 1, :] for n in range(4))
    lam = (jnp.exp(jnp.sum(lq1 * lk1, axis=-1, keepdims=True))
           - jnp.exp(jnp.sum(lq2 * lk2, axis=-1, keepdims=True)) + lam_init)
    outs = []
    for i in range(2):
        cols = slice(i * HEAD_DIM, (i + 1) * HEAD_DIM)
        q = q_ref[0, :, cols] * QK_SCALE

        def step(j, carry, diagonal, q=q, cols=cols):
            rows = pl.ds(pl.multiple_of(j * t, t), t)
            s = _dot_nt(q, k_ref[0, rows, cols]) + key_bias
            if diagonal:
                s = jnp.where(causal, s, MASKED)
            offset = slope * ((qi - j) * t).astype(_F32)
            return _alibi_step(carry, s, v_ref[0, rows, :], offset)

        carry = lax.fori_loop(0, qi, functools.partial(step, diagonal=False), _softmax_init(t, 2 * HEAD_DIM))
        _, l, acc = step(qi, carry, True)
        outs.append(acc / l)
    od = outs[0] - lam * outs[1]
    od = od * lax.rsqrt(jnp.mean(od * od, axis=-1, keepdims=True) + RMS_EPS)
    o_ref[0] = (od * g_ref[...] * (1.0 - lam_init)).astype(o_ref.dtype)


def _dil_kernel(slopes_ref, q_ref, k_ref, v_ref, logm_ref, o_ref):
    t = q_ref.shape[1]
    hp, qi = pl.program_id(1), pl.program_id(2)
    pos = lax.broadcasted_iota(jnp.int32, (1, t), 1).astype(_F32)
    outs = []
    for i in range(HEADS_PER_BLOCK):
        cols = slice(i * HEAD_DIM, (i + 1) * HEAD_DIM)
        slope = slopes_ref[N_DIFF_HEADS + HEADS_PER_BLOCK * hp + i]
        key_bias = slope * pos
        q = q_ref[0, :, cols] * QK_SCALE

        def step(j, carry, q=q, cols=cols, slope=slope, key_bias=key_bias):
            rows = pl.ds(pl.multiple_of(j * t, t), t)
            s = _dot_nt(q, k_ref[0, rows, cols]) + key_bias + logm_ref[qi - j]
            offset = slope * ((qi - j) * t).astype(_F32)
            return _alibi_step(carry, s, v_ref[0, rows, cols], offset)

        _, l, acc = lax.fori_loop(0, qi + 1, step, _softmax_init(t, HEAD_DIM))
        outs.append(acc / l)
    o_ref[0] = jnp.concatenate(outs, axis=-1).astype(o_ref.dtype)


def _qkv_specs(s, seg_q, seg_k, seg_v):
    return [pl.BlockSpec((1, ATT_TQ, LANES), lambda b, g, i: (b, i, seg_q + g)),
            pl.BlockSpec((1, s, LANES), lambda b, g, i: (b, 0, seg_k + g)),
            pl.BlockSpec((1, s, LANES), lambda b, g, i: (b, 0, seg_v + g))]


def _attention_call(kernel, h3, extra_specs, extra_args, segs, num_scalar_prefetch=0, scalars=()):
    b, s, _ = h3.shape
    n_groups = GROUP_WIDTH // LANES
    out_spec = pl.BlockSpec((1, ATT_TQ, LANES), lambda bb, g, i, *_: (bb, i, g))
    in_specs = _qkv_specs(s, *segs) + list(extra_specs)
    if num_scalar_prefetch:
        in_specs = [_with_prefetch(spec) for spec in in_specs]
    grid_spec = pltpu.PrefetchScalarGridSpec(
        num_scalar_prefetch=num_scalar_prefetch,
        grid=(b, n_groups, s // ATT_TQ),
        in_specs=in_specs,
        out_specs=out_spec)
    return pl.pallas_call(
        kernel,
        grid_spec=grid_spec,
        out_shape=jax.ShapeDtypeStruct((b, s, GROUP_WIDTH), _BF16),
        compiler_params=_params("arbitrary", "arbitrary", "arbitrary"),
    )(*scalars, h3, h3, h3, *extra_args)


def _with_prefetch(spec):
    index_map = spec.index_map
    return pl.BlockSpec(spec.block_shape, lambda b, g, i, *_: index_map(b, g, i))


def _log_multiplicity(s):
    nblk = s // ATT_TQ
    d = lax.broadcasted_iota(jnp.int32, (nblk, ATT_TQ, ATT_TK), 0)
    r = lax.broadcasted_iota(jnp.int32, (nblk, ATT_TQ, ATT_TK), 1)
    c = lax.broadcasted_iota(jnp.int32, (nblk, ATT_TQ, ATT_TK), 2)
    dist = d * ATT_TQ + r - c
    mult = jnp.zeros(dist.shape, _F32)
    for window, dilation in DILATED_BRANCHES:
        hit = (dist >= 0) & (dist <= window) & (dist % dilation == 0)
        mult = mult + hit.astype(_F32)
    return jnp.where(mult > 0, jnp.log(jnp.maximum(mult, 1.0)), MASKED)


def _layer_norm(z, g, b):
    mu = jnp.mean(z, axis=-1, keepdims=True)
    zc = z - mu
    var = jnp.mean(zc * zc, axis=-1, keepdims=True)
    return zc * lax.rsqrt(var + LN_EPS) * g + b


def _out_proj_kernel(m0_ref, m1_ref, m2_ref, m3_ref, w_ref, x_ref, g_ref, b_ref, xo_ref, xob_ref):
    y = None
    for n, m_ref in enumerate((m0_ref, m1_ref, m2_ref, m3_ref)):
        part = _dot(m_ref[...], w_ref[n * GROUP_WIDTH:(n + 1) * GROUP_WIDTH, :])
        y = part if y is None else y + part
    out = _layer_norm(DEEPNORM_ALPHA * x_ref[...] + y, g_ref[...], b_ref[...])
    xo_ref[...] = out
    xob_ref[...] = out.astype(_BF16)


def _out_projection(mixed, w, x, g, b, tm=256):
    m, d = x.shape
    row = lambda i: (i, 0)
    const = lambda i: (0, 0)
    return pl.pallas_call(
        _out_proj_kernel,
        grid=(m // tm,),
        in_specs=[pl.BlockSpec((tm, GROUP_WIDTH), row)] * 4 + [
            pl.BlockSpec((d, d), const), pl.BlockSpec((tm, d), row),
            pl.BlockSpec((1, d), const), pl.BlockSpec((1, d), const)],
        out_specs=[pl.BlockSpec((tm, d), row), pl.BlockSpec((tm, d), row)],
        out_shape=[jax.ShapeDtypeStruct((m, d), _F32), jax.ShapeDtypeStruct((m, d), _BF16)],
        compiler_params=_params("arbitrary"),
    )(*mixed, w, x, g, b)


def _mlp_kernel(xb_ref, x_ref, w1_ref, w2_ref, g_ref, b_ref, xo_ref, xob_ref, acc_ref):
    f = pl.program_id(1)

    @pl.when(f == 0)
    def _():
        acc_ref[...] = jnp.zeros_like(acc_ref)

    a = jnp.maximum(_dot(xb_ref[...], w1_ref[...]), 0.0)
    acc_ref[...] += _dot((a * a).astype(_BF16), w2_ref[...])

    @pl.when(f == pl.num_programs(1) - 1)
    def _():
        out = _layer_norm(DEEPNORM_ALPHA * x_ref[...] + acc_ref[...], g_ref[...], b_ref[...])
        xo_ref[...] = out
        xob_ref[...] = out.astype(_BF16)


def _mlp(xb, x, w1, w2, g, b, tm=512, tf=512):
    m, d = x.shape
    ff = w1.shape[1]
    row = lambda i, f: (i, 0)
    const = lambda i, f: (0, 0)
    return pl.pallas_call(
        _mlp_kernel,
        grid=(m // tm, ff // tf),
        in_specs=[pl.BlockSpec((tm, d), row), pl.BlockSpec((tm, d), row),
                  pl.BlockSpec((d, tf), lambda i, f: (0, f)), pl.BlockSpec((tf, d), lambda i, f: (f, 0)),
                  pl.BlockSpec((1, d), const), pl.BlockSpec((1, d), const)],
        out_specs=[pl.BlockSpec((tm, d), row), pl.BlockSpec((tm, d), row)],
        out_shape=[jax.ShapeDtypeStruct((m, d), _F32), jax.ShapeDtypeStruct((m, d), _BF16)],
        scratch_shapes=[pltpu.VMEM((tm, d), _F32)],
        compiler_params=_params("arbitrary", "arbitrary"),
    )(xb, x, w1, w2, g, b)


def _alibi_slopes():
    n = N_DIFF_HEADS + N_DIL_HEADS
    return jnp.asarray(np.exp2(-ALIBI_MAX_EXP * np.arange(1, n + 1) / n), _F32)


def _layer(x, xb, layer_idx, w_main, w_fz, b_fz, lam_vecs, subln_g, w_out, ln1_g, ln1_b,
           w1, w2, ln2_g, ln2_b, slopes, logm, batch):
    m, d = x.shape
    s = m // batch
    h3 = _in_projection(xb, w_main).reshape(batch, s, -1)

    c, ct = _forget_cumsum(xb.reshape(batch, s, d), w_fz, b_fz)
    ct = ct[:, :N_FOX_HEADS].reshape(batch, N_FOX_HEADS, s // ATT_TK, 1, ATT_TK)
    o_fox = _attention_call(
        _fox_kernel, h3,
        [pl.BlockSpec((1, ATT_TQ, LANES), lambda b, g, i: (b, i, 0)),
         pl.BlockSpec((1, HEADS_PER_BLOCK, s // ATT_TK, 1, ATT_TK), lambda b, g, i: (b, g, 0, 0, 0))],
        (c, ct), (_SEG["fq"], _SEG["fk"], _SEG["fv"]))

    o_sb = _attention_call(_sb_kernel, h3, [], (), (_SEG["sq"], _SEG["sk"], _SEG["sv"]))

    lam_init = 0.8 - 0.6 * math.exp(-0.3 * layer_idx)
    o_diff = _attention_call(
        functools.partial(_diff_kernel, lam_init=lam_init), h3,
        [pl.BlockSpec((4, HEAD_DIM), lambda b, g, i: (0, 0)),
         pl.BlockSpec((1, 2 * HEAD_DIM), lambda b, g, i: (0, 0))],
        (lam_vecs, subln_g), (_SEG["dq"], _SEG["dk"], _SEG["dv"]),
        num_scalar_prefetch=1, scalars=(slopes,))

    o_dil = _attention_call(
        _dil_kernel, h3,
        [pl.BlockSpec(logm.shape, lambda b, g, i: (0, 0, 0))],
        (logm,), (_SEG["gq"], _SEG["gk"], _SEG["gv"]),
        num_scalar_prefetch=1, scalars=(slopes,))

    mixed = [o.reshape(m, GROUP_WIDTH) for o in (o_fox, o_sb, o_diff, o_dil)]
    x, xb = _out_projection(mixed, w_out, x, ln1_g, ln1_b)
    return _mlp(xb, x, w1, w2, ln2_g, ln2_b)


def kernel(x, w_in, fox_forget_bias, diff_lambda_q1, diff_lambda_k1, diff_lambda_q2, diff_lambda_k2,
           diff_subln_gain, w_out, ln1_gain, ln1_bias, w_mlp_in, w_mlp_out, ln2_gain, ln2_bias):
    batch, s, d = x.shape
    assert d == D_MODEL and s % ATT_TQ == 0 and ATT_TQ == ATT_TK
    fz0 = 3 * GROUP_WIDTH
    slopes = _alibi_slopes()
    logm = _log_multiplicity(s)
    xf = x.reshape(batch * s, d)
    xb = xf.astype(_BF16)
    for l in range(DEPTH):
        w = w_in[l]
        w_main = jnp.concatenate([w[:, :fz0], w[:, fz0 + N_FOX_HEADS:]], axis=1).astype(_BF16)
        w_fz = jnp.pad(w[:, fz0:fz0 + N_FOX_HEADS], ((0, 0), (0, LANES - N_FOX_HEADS))).astype(_BF16)
        b_fz = jnp.pad(fox_forget_bias[l], (0, LANES - N_FOX_HEADS)).reshape(1, LANES)
        lam_vecs = jnp.stack([diff_lambda_q1[l], diff_lambda_k1[l], diff_lambda_q2[l], diff_lambda_k2[l]])
        xf, xb = _layer(
            xf, xb, l, w_main, w_fz, b_fz, lam_vecs, diff_subln_gain[l].reshape(1, -1),
            w_out[l].astype(_BF16), ln1_gain[l].reshape(1, d), ln1_bias[l].reshape(1, d),
            w_mlp_in[l].astype(_BF16), w_mlp_out[l].astype(_BF16),
            ln2_gain[l].reshape(1, d), ln2_bias[l].reshape(1, d), slopes, logm, batch)
    return xf.reshape(batch, s, d)
```

```python
import functools
import math

import numpy as np
import jax
import jax.numpy as jnp
from jax import lax
from jax.experimental import pallas as pl
from jax.experimental.pallas import tpu as pltpu

D_MODEL = 2048
DEPTH = 2
HEAD_DIM = 64
GROUP_WIDTH = 512
N_FOX_HEADS = 8
N_DIFF_HEADS = 4
N_DIL_HEADS = 8
D_FF = 4 * D_MODEL
DILATED_BRANCHES = ((128, 1), (512, 4), (2048, 16))
ALIBI_MAX_EXP = 8.0
LN_EPS = 1e-5
RMS_EPS = 1e-5
DEEPNORM_ALPHA = (2 * DEPTH) ** 0.25
QK_SCALE = HEAD_DIM ** -0.5

LANES = 128
HEADS_PER_BLOCK = LANES // HEAD_DIM
VMEM_LIMIT_BYTES = 56 * 1024 * 1024
MASKED = -1e30

_SEG = {name: 4 * i for i, name in enumerate(
    ("fq", "fk", "fv", "sq", "sk", "sv", "dq", "dk", "dv", "gq", "gk", "gv"))}

ATT_T = 256

_F32 = jnp.float32
_BF16 = jnp.bfloat16


def _params(*sem):
    return pltpu.CompilerParams(dimension_semantics=sem, vmem_limit_bytes=VMEM_LIMIT_BYTES)


def _dot(a, b):
    return jnp.dot(a, b, preferred_element_type=_F32)


def _dot_nt(a, b):
    return lax.dot_general(a, b, (((1,), (1,)), ((), ())), preferred_element_type=_F32)


def _matmul_kernel(x_ref, w_ref, o_ref):
    o_ref[...] = _dot(x_ref[...], w_ref[...]).astype(o_ref.dtype)


def _in_projection(xb, w, tm=1024, tn=1024):
    m, k = xb.shape
    n = w.shape[1]
    return pl.pallas_call(
        _matmul_kernel,
        grid=(n // tn, m // tm),
        in_specs=[pl.BlockSpec((tm, k), lambda j, i: (i, 0)),
                  pl.BlockSpec((k, tn), lambda j, i: (0, j))],
        out_specs=pl.BlockSpec((tm, tn), lambda j, i: (i, j)),
        out_shape=jax.ShapeDtypeStruct((m, n), _BF16),
        compiler_params=_params("arbitrary", "arbitrary"),
    )(xb, w)


def _split3(x):
    hi = x.astype(_BF16)
    r = x - hi.astype(_F32)
    mid = r.astype(_BF16)
    lo = (r - mid.astype(_F32)).astype(_BF16)
    return hi, mid, lo


def _log_sigmoid_pair(z):
    lp = jnp.log1p(jnp.exp(-jnp.abs(z)))
    return jnp.minimum(z, 0.0) - lp, jnp.minimum(-z, 0.0) - lp


def _forget_kernel(x_ref, w_ref, b_ref, ct_ref, c_scr, *, blk):
    s = x_ref.shape[1]
    z = _dot(x_ref[0], w_ref[...]) + b_ref[...]
    log_f, _ = _log_sigmoid_pair(z)
    r = lax.broadcasted_iota(jnp.int32, (blk, blk), 0)
    c = lax.broadcasted_iota(jnp.int32, (blk, blk), 1)
    tri = (c <= r).astype(_BF16)
    carry = jnp.zeros((1, LANES), _F32)
    for i in range(s // blk):
        hi, mid, lo = _split3(log_f[i * blk:(i + 1) * blk])
        cum = _dot(tri, hi) + _dot(tri, mid) + _dot(tri, lo) + carry
        c_scr[i * blk:(i + 1) * blk, :] = cum
        carry = cum[blk - 1:blk, :]
    ct_ref[0] = c_scr[...].T


def _forget_cumsum(xb3, w_fz, b_fz, blk=256):
    b, s, d = xb3.shape
    return pl.pallas_call(
        functools.partial(_forget_kernel, blk=blk),
        grid=(b,),
        in_specs=[pl.BlockSpec((1, s, d), lambda i: (i, 0, 0)),
                  pl.BlockSpec((d, LANES), lambda i: (0, 0)),
                  pl.BlockSpec((1, LANES), lambda i: (0, 0))],
        out_specs=pl.BlockSpec((1, LANES, s), lambda i: (i, 0, 0)),
        out_shape=jax.ShapeDtypeStruct((b, LANES, s), _F32),
        scratch_shapes=[pltpu.VMEM((s, LANES), _F32)],
        compiler_params=_params("arbitrary"),
    )(xb3, w_fz, b_fz)


def _triangle(t):
    r = lax.broadcasted_iota(jnp.int32, (t, t), 0)
    c = lax.broadcasted_iota(jnp.int32, (t, t), 1)
    return r, c


def _head_lanes(h):
    lane = lax.broadcasted_iota(jnp.int32, (1, LANES), 1)
    return (lane >= h * HEAD_DIM) & (lane < (h + 1) * HEAD_DIM)


def _one_head(q_pair, h):
    return jnp.where(_head_lanes(h), q_pair, jnp.zeros_like(q_pair))


def _causal_probs(s, causal):
    t, n = s.shape
    diag = s[:, n - t:] if causal is None else jnp.where(causal, s[:, n - t:], MASKED)
    m = jnp.max(diag, axis=-1, keepdims=True)
    if n > t:
        left = s[:, :n - t]
        m = jnp.maximum(m, jnp.max(left, axis=-1, keepdims=True))
    p_diag = jnp.exp(diag - m)
    l = jnp.sum(p_diag, axis=-1, keepdims=True)
    if n == t:
        return p_diag, l
    p_left = jnp.exp(left - m)
    l = l + jnp.sum(p_left, axis=-1, keepdims=True)
    return jnp.concatenate([p_left, p_diag], axis=-1), l


def _fox_kernel(q_ref, k_ref, v_ref, ct_ref, o_ref):
    t = ATT_T
    s_len = q_ref.shape[1]
    row, col = _triangle(t)
    causal = row >= col
    first = _head_lanes(0)
    for i in range(s_len // t):
        n = (i + 1) * t
        q_pair = q_ref[0, i * t:n, :] * QK_SCALE
        outs = []
        for h in range(HEADS_PER_BLOCK):
            c_row = ct_ref[0, h, :, :n]
            key_bias = c_row[:, n - 1:n] - c_row
            s = _dot_nt(_one_head(q_pair, h), k_ref[0, :n, :]) + key_bias
            p, l = _causal_probs(s, causal)
            outs.append(_dot(p.astype(_BF16), v_ref[0, :n, :]) / l)
        o_ref[0, i * t:n, :] = jnp.where(first, outs[0], outs[1]).astype(o_ref.dtype)


def _sb_kernel(q_ref, k_ref, v_ref, o_ref):
    t = ATT_T
    s_len = q_ref.shape[1]
    row, col = _triangle(t)
    strict = row > col
    later_ones = strict.astype(_BF16)
    first = _head_lanes(0)
    for i in range(s_len // t):
        n = (i + 1) * t
        q_pair = q_ref[0, i * t:n, :] * QK_SCALE
        outs = []
        for h in range(HEADS_PER_BLOCK):
            z = _dot_nt(_one_head(q_pair, h), k_ref[0, :n, :])
            log_beta, log_rest = _log_sigmoid_pair(z)
            tail = jnp.zeros((t, 1), _F32)
            chunks = [None] * (i + 1)
            for c in range(i, -1, -1):
                lr = log_rest[:, c * t:(c + 1) * t]
                if c == i:
                    lr = jnp.where(strict, lr, 0.0)
                hi = lr.astype(_BF16)
                lo = (lr - hi.astype(_F32)).astype(_BF16)
                later = _dot(hi, later_ones) + _dot(lo, later_ones)
                a = jnp.exp(log_beta[:, c * t:(c + 1) * t] + later + tail)
                if c == i:
                    a = jnp.where(strict, a, 0.0)
                chunks[c] = a.astype(_BF16)
                tail = tail + jnp.sum(lr, axis=-1, keepdims=True)
            a_all = chunks[0] if i == 0 else jnp.concatenate(chunks, axis=-1)
            outs.append(_dot(a_all, v_ref[0, :n, :]))
        o_ref[0, i * t:n, :] = jnp.where(first, outs[0], outs[1]).astype(o_ref.dtype)


def _diff_kernel(slopes_ref, q_ref, k_ref, v_ref, lam_ref, g_ref, o_ref, *, lam_init):
    t = ATT_T
    s_len = q_ref.shape[1]
    slope = slopes_ref[pl.program_id(1)]
    row, col = _triangle(t)
    causal = row >= col
    lq1, lk1, lq2, lk2 = (lam_ref[n:n + 1, :] for n in range(4))
    lam = (jnp.exp(jnp.sum(lq1 * lk1, axis=-1, keepdims=True))
           - jnp.exp(jnp.sum(lq2 * lk2, axis=-1, keepdims=True)) + lam_init)
    gain = g_ref[...] * (1.0 - lam_init)
    for i in range(s_len // t):
        n = (i + 1) * t
        q_pair = q_ref[0, i * t:n, :] * QK_SCALE
        key_bias = slope * (lax.broadcasted_iota(jnp.int32, (1, n), 1) - (n - 1)).astype(_F32)
        p1, l1 = _causal_probs(_dot_nt(_one_head(q_pair, 0), k_ref[0, :n, :]) + key_bias, causal)
        p2, l2 = _causal_probs(_dot_nt(_one_head(q_pair, 1), k_ref[0, :n, :]) + key_bias, causal)
        a = p1 * (1.0 / l1) - p2 * (lam / l2)
        od = _dot(a.astype(_BF16), v_ref[0, :n, :])
        od = od * lax.rsqrt(jnp.mean(od * od, axis=-1, keepdims=True) + RMS_EPS)
        o_ref[0, i * t:n, :] = (od * gain).astype(o_ref.dtype)


def _dil_kernel(slopes_ref, q_ref, k_ref, v_ref, logm_ref, ndist_ref, o_ref):
    t = ATT_T
    s_len = q_ref.shape[1]
    first = _head_lanes(0)
    for i in range(s_len // t):
        n = (i + 1) * t
        q_pair = q_ref[0, i * t:n, :] * QK_SCALE
        outs = []
        for h in range(HEADS_PER_BLOCK):
            slope = slopes_ref[N_DIFF_HEADS + HEADS_PER_BLOCK * pl.program_id(1) + h]
            bias = logm_ref[:, s_len - n:] + slope * ndist_ref[:, s_len - n:]
            s = _dot_nt(_one_head(q_pair, h), k_ref[0, :n, :]) + bias
            p, l = _causal_probs(s, None)
            outs.append(_dot(p.astype(_BF16), v_ref[0, :n, :]) / l)
        o_ref[0, i * t:n, :] = jnp.where(first, outs[0], outs[1]).astype(o_ref.dtype)


def _attention_call(kernel, h3, extra_specs, extra_args, segs, scalars=()):
    b, s, _ = h3.shape
    seg_q, seg_k, seg_v = segs

    def col_block(seg):
        return pl.BlockSpec((1, s, LANES), lambda bb, g, *_: (bb, 0, seg + g))

    grid_spec = pltpu.PrefetchScalarGridSpec(
        num_scalar_prefetch=len(scalars),
        grid=(b, GROUP_WIDTH // LANES),
        in_specs=[col_block(seg_q), col_block(seg_k), col_block(seg_v)] + list(extra_specs),
        out_specs=col_block(0))
    return pl.pallas_call(
        kernel,
        grid_spec=grid_spec,
        out_shape=jax.ShapeDtypeStruct((b, s, GROUP_WIDTH), _BF16),
        compiler_params=_params("arbitrary", "arbitrary"),
    )(*scalars, h3, h3, h3, *extra_args)


def _dilated_tables(s):
    r = lax.broadcasted_iota(jnp.int32, (ATT_T, s), 0)
    c = lax.broadcasted_iota(jnp.int32, (ATT_T, s), 1)
    dist = (s - ATT_T) + r - c
    mult = jnp.zeros(dist.shape, _F32)
    for window, dilation in DILATED_BRANCHES:
        hit = (dist >= 0) & (dist <= window) & (dist % dilation == 0)
        mult = mult + hit.astype(_F32)
    logm = jnp.where(mult > 0, jnp.log(jnp.maximum(mult, 1.0)), MASKED)
    return logm, -jnp.maximum(dist, 0).astype(_F32)


def _layer_norm(z, g, b):
    mu = jnp.mean(z, axis=-1, keepdims=True)
    zc = z - mu
    var = jnp.mean(zc * zc, axis=-1, keepdims=True)
    return zc * lax.rsqrt(var + LN_EPS) * g + b


def _out_proj_kernel(m0_ref, m1_ref, m2_ref, m3_ref, w_ref, x_ref, g_ref, b_ref, xo_ref, xob_ref):
    y = None
    for n, m_ref in enumerate((m0_ref, m1_ref, m2_ref, m3_ref)):
        part = _dot(m_ref[...], w_ref[n * GROUP_WIDTH:(n + 1) * GROUP_WIDTH, :])
        y = part if y is None else y + part
    out = _layer_norm(DEEPNORM_ALPHA * x_ref[...] + y, g_ref[...], b_ref[...])
    xo_ref[...] = out
    xob_ref[...] = out.astype(_BF16)


def _out_projection(mixed, w, x, g, b, tm=256):
    m, d = x.shape
    row = lambda i: (i, 0)
    const = lambda i: (0, 0)
    return pl.pallas_call(
        _out_proj_kernel,
        grid=(m // tm,),
        in_specs=[pl.BlockSpec((tm, GROUP_WIDTH), row)] * 4 + [
            pl.BlockSpec((d, d), const), pl.BlockSpec((tm, d), row),
            pl.BlockSpec((1, d), const), pl.BlockSpec((1, d), const)],
        out_specs=[pl.BlockSpec((tm, d), row), pl.BlockSpec((tm, d), row)],
        out_shape=[jax.ShapeDtypeStruct((m, d), _F32), jax.ShapeDtypeStruct((m, d), _BF16)],
        compiler_params=_params("arbitrary"),
    )(*mixed, w, x, g, b)


def _mlp_kernel(xb_ref, x_ref, w1_ref, w2_ref, g_ref, b_ref, xo_ref, xob_ref, acc_ref):
    f = pl.program_id(1)

    @pl.when(f == 0)
    def _():
        acc_ref[...] = jnp.zeros_like(acc_ref)

    a = jnp.maximum(_dot(xb_ref[...], w1_ref[...]), 0.0)
    acc_ref[...] += _dot((a * a).astype(_BF16), w2_ref[...])

    @pl.when(f == pl.num_programs(1) - 1)
    def _():
        out = _layer_norm(DEEPNORM_ALPHA * x_ref[...] + acc_ref[...], g_ref[...], b_ref[...])
        xo_ref[...] = out
        xob_ref[...] = out.astype(_BF16)


def _mlp(xb, x, w1, w2, g, b, tm=512, tf=512):
    m, d = x.shape
    ff = w1.shape[1]
    row = lambda i, f: (i, 0)
    const = lambda i, f: (0, 0)
    return pl.pallas_call(
        _mlp_kernel,
        grid=(m // tm, ff // tf),
        in_specs=[pl.BlockSpec((tm, d), row), pl.BlockSpec((tm, d), row),
                  pl.BlockSpec((d, tf), lambda i, f: (0, f)), pl.BlockSpec((tf, d), lambda i, f: (f, 0)),
                  pl.BlockSpec((1, d), const), pl.BlockSpec((1, d), const)],
        out_specs=[pl.BlockSpec((tm, d), row), pl.BlockSpec((tm, d), row)],
        out_shape=[jax.ShapeDtypeStruct((m, d), _F32), jax.ShapeDtypeStruct((m, d), _BF16)],
        scratch_shapes=[pltpu.VMEM((tm, d), _F32)],
        compiler_params=_params("arbitrary", "arbitrary"),
    )(xb, x, w1, w2, g, b)


def _alibi_slopes():
    n = N_DIFF_HEADS + N_DIL_HEADS
    return jnp.asarray(np.exp2(-ALIBI_MAX_EXP * np.arange(1, n + 1) / n), _F32)


def _layer(x, xb, layer_idx, w_main, w_fz, b_fz, lam_vecs, subln_g, w_out, ln1_g, ln1_b,
           w1, w2, ln2_g, ln2_b, slopes, dil_tables, batch):
    m, d = x.shape
    s = m // batch
    h3 = _in_projection(xb, w_main).reshape(batch, s, -1)

    ct = _forget_cumsum(xb.reshape(batch, s, d), w_fz, b_fz)
    ct = ct[:, :N_FOX_HEADS].reshape(batch, N_FOX_HEADS, 1, s)
    o_fox = _attention_call(
        _fox_kernel, h3,
        [pl.BlockSpec((1, HEADS_PER_BLOCK, 1, s), lambda b, g: (b, g, 0, 0))],
        (ct,), (_SEG["fq"], _SEG["fk"], _SEG["fv"]))

    o_sb = _attention_call(_sb_kernel, h3, [], (), (_SEG["sq"], _SEG["sk"], _SEG["sv"]))

    lam_init = 0.8 - 0.6 * math.exp(-0.3 * layer_idx)
    o_diff = _attention_call(
        functools.partial(_diff_kernel, lam_init=lam_init), h3,
        [pl.BlockSpec((4, HEAD_DIM), lambda b, g, *_: (0, 0)),
         pl.BlockSpec((1, 2 * HEAD_DIM), lambda b, g, *_: (0, 0))],
        (lam_vecs, subln_g), (_SEG["dq"], _SEG["dk"], _SEG["dv"]), scalars=(slopes,))

    o_dil = _attention_call(
        _dil_kernel, h3,
        [pl.BlockSpec((ATT_T, s), lambda b, g, *_: (0, 0))] * 2,
        dil_tables, (_SEG["gq"], _SEG["gk"], _SEG["gv"]), scalars=(slopes,))

    mixed = [o.reshape(m, GROUP_WIDTH) for o in (o_fox, o_sb, o_diff, o_dil)]
    x, xb = _out_projection(mixed, w_out, x, ln1_g, ln1_b)
    return _mlp(xb, x, w1, w2, ln2_g, ln2_b)


def kernel(x, w_in, fox_forget_bias, diff_lambda_q1, diff_lambda_k1, diff_lambda_q2, diff_lambda_k2,
           diff_subln_gain, w_out, ln1_gain, ln1_bias, w_mlp_in, w_mlp_out, ln2_gain, ln2_bias):
    batch, s, d = x.shape
    assert d == D_MODEL and s % ATT_T == 0
    fz0 = 3 * GROUP_WIDTH
    slopes = _alibi_slopes()
    dil_tables = _dilated_tables(s)
    xf = x.reshape(batch * s, d)
    xb = xf.astype(_BF16)
    for l in range(DEPTH):
        w = w_in[l]
        w_main = jnp.concatenate([w[:, :fz0], w[:, fz0 + N_FOX_HEADS:]], axis=1).astype(_BF16)
        w_fz = jnp.pad(w[:, fz0:fz0 + N_FOX_HEADS], ((0, 0), (0, LANES - N_FOX_HEADS))).astype(_BF16)
        b_fz = jnp.pad(fox_forget_bias[l], (0, LANES - N_FOX_HEADS)).reshape(1, LANES)
        lam_vecs = jnp.stack([diff_lambda_q1[l], diff_lambda_k1[l], diff_lambda_q2[l], diff_lambda_k2[l]])
        xf, xb = _layer(
            xf, xb, l, w_main, w_fz, b_fz, lam_vecs, diff_subln_gain[l].reshape(1, -1),
            w_out[l].astype(_BF16), ln1_gain[l].reshape(1, d), ln1_bias[l].reshape(1, d),
            w_mlp_in[l].astype(_BF16), w_mlp_out[l].astype(_BF16),
            ln2_gain[l].reshape(1, d), ln2_bias[l].reshape(1, d), slopes, dil_tables, batch)
    return xf.reshape(batch, s, d)
```

```python
import functools
import math

import numpy as np
import jax
import jax.numpy as jnp
from jax import lax
from jax.experimental import pallas as pl
from jax.experimental.pallas import tpu as pltpu

D_MODEL = 2048
DEPTH = 2
HEAD_DIM = 64
GROUP_WIDTH = 512
N_FOX_HEADS = 8
N_DIFF_HEADS = 4
N_DIL_HEADS = 8
D_FF = 4 * D_MODEL
DILATED_BRANCHES = ((128, 1), (512, 4), (2048, 16))
ALIBI_MAX_EXP = 8.0
LN_EPS = 1e-5
RMS_EPS = 1e-5
DEEPNORM_ALPHA = (2 * DEPTH) ** 0.25
QK_SCALE = HEAD_DIM ** -0.5

LANES = 128
HEADS_PER_BLOCK = LANES // HEAD_DIM
VMEM_LIMIT_BYTES = 56 * 1024 * 1024
MASKED = -1e30

_SEG = {name: 4 * i for i, name in enumerate(
    ("fq", "fk", "fv", "sq", "sk", "sv", "dq", "dk", "dv", "gq", "gk", "gv"))}

ATT_T = 256

_F32 = jnp.float32
_BF16 = jnp.bfloat16


def _params(*sem):
    return pltpu.CompilerParams(dimension_semantics=sem, vmem_limit_bytes=VMEM_LIMIT_BYTES)


def _dot(a, b):
    return jnp.dot(a, b, preferred_element_type=_F32)


def _dot_nt(a, b):
    return lax.dot_general(a, b, (((1,), (1,)), ((), ())), preferred_element_type=_F32)


def _matmul_kernel(x_ref, w_ref, o_ref):
    o_ref[...] = _dot(x_ref[...], w_ref[...]).astype(o_ref.dtype)


def _in_projection(xb, w, layer, tm=1024, tn=1024):
    m, k = xb.shape
    n = w.shape[2]
    return pl.pallas_call(
        _matmul_kernel,
        grid=(n // tn, m // tm),
        in_specs=[pl.BlockSpec((tm, k), lambda j, i: (i, 0)),
                  pl.BlockSpec((None, k, tn), lambda j, i: (layer, 0, j))],
        out_specs=pl.BlockSpec((tm, tn), lambda j, i: (i, j)),
        out_shape=jax.ShapeDtypeStruct((m, n), _BF16),
        compiler_params=_params("arbitrary", "arbitrary"),
    )(xb, w)


def _repack_kernel(w_ref, wm_ref, wf_ref):
    w = w_ref[0]
    fz0 = 3 * GROUP_WIDTH
    wm_ref[0, :, :fz0] = w[:, :fz0].astype(_BF16)
    wm_ref[0, :, fz0:] = w[:, fz0 + N_FOX_HEADS:].astype(_BF16)
    lane = lax.broadcasted_iota(jnp.int32, (1, LANES), 1)
    wf_ref[0] = jnp.where(lane < N_FOX_HEADS, w[:, fz0:fz0 + LANES], 0.0).astype(_BF16)


def _repack_in_projection(w_in, tk=256):
    depth, k, n = w_in.shape
    n_main = n - N_FOX_HEADS
    return pl.pallas_call(
        _repack_kernel,
        grid=(depth, k // tk),
        in_specs=[pl.BlockSpec((1, tk, n), lambda l, i: (l, i, 0))],
        out_specs=[pl.BlockSpec((1, tk, n_main), lambda l, i: (l, i, 0)),
                   pl.BlockSpec((1, tk, LANES), lambda l, i: (l, i, 0))],
        out_shape=[jax.ShapeDtypeStruct((depth, k, n_main), _BF16),
                   jax.ShapeDtypeStruct((depth, k, LANES), _BF16)],
        compiler_params=_params("arbitrary", "arbitrary"),
    )(w_in)


def _split3(x):
    hi = x.astype(_BF16)
    r = x - hi.astype(_F32)
    mid = r.astype(_BF16)
    lo = (r - mid.astype(_F32)).astype(_BF16)
    return hi, mid, lo


def _log_sigmoid_pair(z):
    log_sig = jnp.minimum(z, 0.0) - jnp.log(1.0 + jnp.exp(-jnp.abs(z)))
    return log_sig, log_sig - z


def _forget_kernel(x_ref, w_ref, b_ref, ct_ref, c_scr, *, blk):
    s = x_ref.shape[1]
    z = _dot(x_ref[0], w_ref[...]) + b_ref[...]
    log_f, _ = _log_sigmoid_pair(z)
    r = lax.broadcasted_iota(jnp.int32, (blk, blk), 0)
    c = lax.broadcasted_iota(jnp.int32, (blk, blk), 1)
    tri = (c <= r).astype(_BF16)
    carry = jnp.zeros((1, LANES), _F32)
    for i in range(s // blk):
        hi, mid, lo = _split3(log_f[i * blk:(i + 1) * blk])
        cum = _dot(tri, hi) + _dot(tri, mid) + _dot(tri, lo) + carry
        c_scr[i * blk:(i + 1) * blk, :] = cum
        carry = cum[blk - 1:blk, :]
    ct_ref[0] = c_scr[...].T[:N_FOX_HEADS]


def _forget_cumsum(xb3, w_fz, b_fz, layer, blk=256):
    b, s, d = xb3.shape
    return pl.pallas_call(
        functools.partial(_forget_kernel, blk=blk),
        grid=(b,),
        in_specs=[pl.BlockSpec((1, s, d), lambda i: (i, 0, 0)),
                  pl.BlockSpec((None, d, LANES), lambda i: (layer, 0, 0)),
                  pl.BlockSpec((1, LANES), lambda i: (0, 0))],
        out_specs=pl.BlockSpec((1, N_FOX_HEADS, s), lambda i: (i, 0, 0)),
        out_shape=jax.ShapeDtypeStruct((b, N_FOX_HEADS, s), _F32),
        scratch_shapes=[pltpu.VMEM((s, LANES), _F32)],
        compiler_params=_params("arbitrary"),
    )(xb3, w_fz, b_fz)


def _triangle(t):
    r = lax.broadcasted_iota(jnp.int32, (t, t), 0)
    c = lax.broadcasted_iota(jnp.int32, (t, t), 1)
    return r, c


def _head_lanes(h):
    lane = lax.broadcasted_iota(jnp.int32, (1, LANES), 1)
    return (lane >= h * HEAD_DIM) & (lane < (h + 1) * HEAD_DIM)


def _one_head(q_pair, h):
    return jnp.where(_head_lanes(h), q_pair, jnp.zeros_like(q_pair))


def _causal_probs(s, causal, with_sums):
    t, n = s.shape
    diag = s[:, n - t:] if causal is None else jnp.where(causal, s[:, n - t:], MASKED)
    m = jnp.max(diag, axis=-1, keepdims=True)
    if n > t:
        left = s[:, :n - t]
        m = jnp.maximum(m, jnp.max(left, axis=-1, keepdims=True))
    p_diag = jnp.exp(diag - m)
    l = jnp.sum(p_diag, axis=-1, keepdims=True) if with_sums else None
    if n == t:
        return p_diag, l
    p_left = jnp.exp(left - m)
    if with_sums:
        l = l + jnp.sum(p_left, axis=-1, keepdims=True)
    return jnp.concatenate([p_left, p_diag], axis=-1), l


def _values_with_ones(v_pair, h):
    return jnp.where(_head_lanes(h), v_pair, jnp.ones_like(v_pair))


def _normalise(o):
    return o / pltpu.roll(o, HEAD_DIM, 1)


def _fox_kernel(q_ref, k_ref, v_ref, ct_ref, o_ref):
    t = ATT_T
    s_len = q_ref.shape[1]
    row, col = _triangle(t)
    causal = row >= col
    first = _head_lanes(0)
    values = [_values_with_ones(v_ref[0], h) for h in range(HEADS_PER_BLOCK)]
    for i in range(s_len // t):
        n = (i + 1) * t
        q_pair = q_ref[0, i * t:n, :] * QK_SCALE
        outs = []
        for h in range(HEADS_PER_BLOCK):
            c_row = ct_ref[0, h, :, :n]
            key_bias = c_row[:, n - 1:n] - c_row
            s = _dot_nt(_one_head(q_pair, h), k_ref[0, :n, :]) + key_bias
            p, _ = _causal_probs(s, causal, False)
            outs.append(_normalise(_dot(p.astype(_BF16), values[h][:n])))
        o_ref[0, i * t:n, :] = jnp.where(first, outs[0], outs[1]).astype(o_ref.dtype)


def _sb_kernel(q_ref, k_ref, v_ref, o_ref):
    t = ATT_T
    s_len = q_ref.shape[1]
    row, col = _triangle(t)
    strict = row > col
    later_ones = strict.astype(_BF16)
    first = _head_lanes(0)
    for i in range(s_len // t):
        n = (i + 1) * t
        q_pair = q_ref[0, i * t:n, :] * QK_SCALE
        outs = []
        for h in range(HEADS_PER_BLOCK):
            z = _dot_nt(_one_head(q_pair, h), k_ref[0, :n, :])
            log_beta, log_rest = _log_sigmoid_pair(z)
            tail = jnp.zeros((t, 1), _F32)
            chunks = [None] * (i + 1)
            for c in range(i, -1, -1):
                lr = log_rest[:, c * t:(c + 1) * t]
                if c == i:
                    lr = jnp.where(strict, lr, 0.0)
                hi = lr.astype(_BF16)
                lo = (lr - hi.astype(_F32)).astype(_BF16)
                later = _dot(hi, later_ones) + _dot(lo, later_ones)
                a = jnp.exp(log_beta[:, c * t:(c + 1) * t] + later + tail)
                if c == i:
                    a = jnp.where(strict, a, 0.0)
                chunks[c] = a.astype(_BF16)
                tail = tail + jnp.sum(lr, axis=-1, keepdims=True)
            a_all = chunks[0] if i == 0 else jnp.concatenate(chunks, axis=-1)
            outs.append(_dot(a_all, v_ref[0, :n, :]))
        o_ref[0, i * t:n, :] = jnp.where(first, outs[0], outs[1]).astype(o_ref.dtype)


def _diff_kernel(slopes_ref, q_ref, k_ref, v_ref, lam_ref, g_ref, o_ref, *, lam_init):
    t = ATT_T
    s_len = q_ref.shape[1]
    slope = slopes_ref[pl.program_id(1)]
    row, col = _triangle(t)
    causal = row >= col
    lq1, lk1, lq2, lk2 = (lam_ref[n:n + 1, :] for n in range(4))
    lam = (jnp.exp(jnp.sum(lq1 * lk1, axis=-1, keepdims=True))
           - jnp.exp(jnp.sum(lq2 * lk2, axis=-1, keepdims=True)) + lam_init)
    gain = g_ref[...] * (1.0 - lam_init)
    for i in range(s_len // t):
        n = (i + 1) * t
        q_pair = q_ref[0, i * t:n, :] * QK_SCALE
        key_bias = slope * (lax.broadcasted_iota(jnp.int32, (1, n), 1) - (n - 1)).astype(_F32)
        p1, l1 = _causal_probs(_dot_nt(_one_head(q_pair, 0), k_ref[0, :n, :]) + key_bias, causal, True)
        p2, l2 = _causal_probs(_dot_nt(_one_head(q_pair, 1), k_ref[0, :n, :]) + key_bias, causal, True)
        a = p1 * (1.0 / l1) - p2 * (lam / l2)
        od = _dot(a.astype(_BF16), v_ref[0, :n, :])
        od = od * lax.rsqrt(jnp.mean(od * od, axis=-1, keepdims=True) + RMS_EPS)
        o_ref[0, i * t:n, :] = (od * gain).astype(o_ref.dtype)


def _dil_kernel(slopes_ref, q_ref, k_ref, v_ref, logm_ref, ndist_ref, o_ref, bias_ref):
    t = ATT_T
    s_len = q_ref.shape[1]
    first = _head_lanes(0)
    values = [_values_with_ones(v_ref[0], h) for h in range(HEADS_PER_BLOCK)]
    for h in range(HEADS_PER_BLOCK):
        slope = slopes_ref[N_DIFF_HEADS + HEADS_PER_BLOCK * pl.program_id(1) + h]
        bias_ref[h] = logm_ref[...] + slope * ndist_ref[...]
    for i in range(s_len // t):
        n = (i + 1) * t
        q_pair = q_ref[0, i * t:n, :] * QK_SCALE
        outs = []
        for h in range(HEADS_PER_BLOCK):
            s = _dot_nt(_one_head(q_pair, h), k_ref[0, :n, :]) + bias_ref[h, :, s_len - n:]
            p, _ = _causal_probs(s, None, False)
            outs.append(_normalise(_dot(p.astype(_BF16), values[h][:n])))
        o_ref[0, i * t:n, :] = jnp.where(first, outs[0], outs[1]).astype(o_ref.dtype)


def _attention_call(kernel, h3, extra_specs, extra_args, segs, scalars=(), scratch_shapes=()):
    b, s, _ = h3.shape
    seg_q, seg_k, seg_v = segs

    def col_block(seg):
        return pl.BlockSpec((1, s, LANES), lambda bb, g, *_: (bb, 0, seg + g))

    grid_spec = pltpu.PrefetchScalarGridSpec(
        num_scalar_prefetch=len(scalars),
        grid=(b, GROUP_WIDTH // LANES),
        in_specs=[col_block(seg_q), col_block(seg_k), col_block(seg_v)] + list(extra_specs),
        out_specs=col_block(0),
        scratch_shapes=list(scratch_shapes))
    return pl.pallas_call(
        kernel,
        grid_spec=grid_spec,
        out_shape=jax.ShapeDtypeStruct((b, s, GROUP_WIDTH), _BF16),
        compiler_params=_params("arbitrary", "arbitrary"),
    )(*scalars, h3, h3, h3, *extra_args)


def _dilated_tables(s):
    r = lax.broadcasted_iota(jnp.int32, (ATT_T, s), 0)
    c = lax.broadcasted_iota(jnp.int32, (ATT_T, s), 1)
    dist = (s - ATT_T) + r - c
    mult = jnp.zeros(dist.shape, _F32)
    for window, dilation in DILATED_BRANCHES:
        hit = (dist >= 0) & (dist <= window) & (dist % dilation == 0)
        mult = mult + hit.astype(_F32)
    logm = jnp.where(mult > 0, jnp.log(jnp.maximum(mult, 1.0)), MASKED)
    return logm, -jnp.maximum(dist, 0).astype(_F32)


def _layer_norm(z, g, b):
    mu = jnp.mean(z, axis=-1, keepdims=True)
    zc = z - mu
    var = jnp.mean(zc * zc, axis=-1, keepdims=True)
    return zc * lax.rsqrt(var + LN_EPS) * g + b


def _out_proj_kernel(m0_ref, m1_ref, m2_ref, m3_ref, w_ref, x_ref, g_ref, b_ref, xo_ref, xob_ref):
    y = None
    for n, m_ref in enumerate((m0_ref, m1_ref, m2_ref, m3_ref)):
        part = _dot(m_ref[...], w_ref[n * GROUP_WIDTH:(n + 1) * GROUP_WIDTH, :])
        y = part if y is None else y + part
    out = _layer_norm(DEEPNORM_ALPHA * x_ref[...] + y, g_ref[...], b_ref[...])
    xo_ref[...] = out
    xob_ref[...] = out.astype(_BF16)


def _out_projection(mixed, w, x, g, b, layer, tm=256):
    m, d = x.shape
    row = lambda i: (i, 0)
    const = lambda i: (0, 0)
    return pl.pallas_call(
        _out_proj_kernel,
        grid=(m // tm,),
        in_specs=[pl.BlockSpec((tm, GROUP_WIDTH), row)] * 4 + [
            pl.BlockSpec((None, d, d), lambda i: (layer, 0, 0)), pl.BlockSpec((tm, d), row),
            pl.BlockSpec((1, d), const), pl.BlockSpec((1, d), const)],
        out_specs=[pl.BlockSpec((tm, d), row), pl.BlockSpec((tm, d), row)],
        out_shape=[jax.ShapeDtypeStruct((m, d), _F32), jax.ShapeDtypeStruct((m, d), _BF16)],
        compiler_params=_params("arbitrary"),
    )(*mixed, w, x, g, b)


def _mlp_kernel(xb_ref, x_ref, w1_ref, w2_ref, g_ref, b_ref, xo_ref, xob_ref, acc_ref):
    f = pl.program_id(1)

    @pl.when(f == 0)
    def _():
        acc_ref[...] = jnp.zeros_like(acc_ref)

    a = jnp.maximum(_dot(xb_ref[...], w1_ref[...]), 0.0)
    acc_ref[...] += _dot((a * a).astype(_BF16), w2_ref[...])

    @pl.when(f == pl.num_programs(1) - 1)
    def _():
        out = _layer_norm(DEEPNORM_ALPHA * x_ref[...] + acc_ref[...], g_ref[...], b_ref[...])
        xo_ref[...] = out
        xob_ref[...] = out.astype(_BF16)


def _mlp(xb, x, w1, w2, g, b, layer, tm=512, tf=1024):
    m, d = x.shape
    ff = w1.shape[2]
    row = lambda i, f: (i, 0)
    const = lambda i, f: (0, 0)
    return pl.pallas_call(
        _mlp_kernel,
        grid=(m // tm, ff // tf),
        in_specs=[pl.BlockSpec((tm, d), row), pl.BlockSpec((tm, d), row),
                  pl.BlockSpec((None, d, tf), lambda i, f: (layer, 0, f)),
                  pl.BlockSpec((None, tf, d), lambda i, f: (layer, f, 0)),
                  pl.BlockSpec((1, d), const), pl.BlockSpec((1, d), const)],
        out_specs=[pl.BlockSpec((tm, d), row), pl.BlockSpec((tm, d), row)],
        out_shape=[jax.ShapeDtypeStruct((m, d), _F32), jax.ShapeDtypeStruct((m, d), _BF16)],
        scratch_shapes=[pltpu.VMEM((tm, d), _F32)],
        compiler_params=_params("arbitrary", "arbitrary"),
    )(xb, x, w1, w2, g, b)


def _alibi_slopes():
    n = N_DIFF_HEADS + N_DIL_HEADS
    return jnp.asarray(np.exp2(-ALIBI_MAX_EXP * np.arange(1, n + 1) / n), _F32)


def _layer(x, xb, layer_idx, w_main, w_fz, b_fz, lam_vecs, subln_g, w_out, ln1_g, ln1_b,
           w1, w2, ln2_g, ln2_b, slopes, dil_tables, batch):
    m, d = x.shape
    s = m // batch
    h3 = _in_projection(xb, w_main, layer_idx).reshape(batch, s, -1)

    ct = _forget_cumsum(xb.reshape(batch, s, d), w_fz, b_fz, layer_idx)
    ct = ct.reshape(batch, N_FOX_HEADS, 1, s)
    o_fox = _attention_call(
        _fox_kernel, h3,
        [pl.BlockSpec((1, HEADS_PER_BLOCK, 1, s), lambda b, g: (b, g, 0, 0))],
        (ct,), (_SEG["fq"], _SEG["fk"], _SEG["fv"]))

    o_sb = _attention_call(_sb_kernel, h3, [], (), (_SEG["sq"], _SEG["sk"], _SEG["sv"]))

    lam_init = 0.8 - 0.6 * math.exp(-0.3 * layer_idx)
    o_diff = _attention_call(
        functools.partial(_diff_kernel, lam_init=lam_init), h3,
        [pl.BlockSpec((4, HEAD_DIM), lambda b, g, *_: (0, 0)),
         pl.BlockSpec((1, 2 * HEAD_DIM), lambda b, g, *_: (0, 0))],
        (lam_vecs, subln_g), (_SEG["dq"], _SEG["dk"], _SEG["dv"]), scalars=(slopes,))

    o_dil = _attention_call(
        _dil_kernel, h3,
        [pl.BlockSpec((ATT_T, s), lambda b, g, *_: (0, 0))] * 2,
        dil_tables, (_SEG["gq"], _SEG["gk"], _SEG["gv"]), scalars=(slopes,),
        scratch_shapes=[pltpu.VMEM((HEADS_PER_BLOCK, ATT_T, s), _F32)])

    mixed = [o.reshape(m, GROUP_WIDTH) for o in (o_fox, o_sb, o_diff, o_dil)]
    x, xb = _out_projection(mixed, w_out, x, ln1_g, ln1_b, layer_idx)
    return _mlp(xb, x, w1, w2, ln2_g, ln2_b, layer_idx)


def kernel(x, w_in, fox_forget_bias, diff_lambda_q1, diff_lambda_k1, diff_lambda_q2, diff_lambda_k2,
           diff_subln_gain, w_out, ln1_gain, ln1_bias, w_mlp_in, w_mlp_out, ln2_gain, ln2_bias):
    batch, s, d = x.shape
    assert d == D_MODEL and s % ATT_T == 0
    slopes = _alibi_slopes()
    dil_tables = _dilated_tables(s)
    xf = x.reshape(batch * s, d)
    xb = xf.astype(_BF16)
    w_main, w_fz = _repack_in_projection(w_in)
    w_out, w_mlp_in, w_mlp_out = (w.astype(_BF16) for w in (w_out, w_mlp_in, w_mlp_out))
    for l in range(DEPTH):
        b_fz = jnp.pad(fox_forget_bias[l], (0, LANES - N_FOX_HEADS)).reshape(1, LANES)
        lam_vecs = jnp.stack([diff_lambda_q1[l], diff_lambda_k1[l], diff_lambda_q2[l], diff_lambda_k2[l]])
        xf, xb = _layer(
            xf, xb, l, w_main, w_fz, b_fz, lam_vecs, diff_subln_gain[l].reshape(1, -1),
            w_out, ln1_gain[l].reshape(1, d), ln1_bias[l].reshape(1, d),
            w_mlp_in, w_mlp_out,
            ln2_gain[l].reshape(1, d), ln2_bias[l].reshape(1, d), slopes, dil_tables, batch)
    return xf.reshape(batch, s, d)
```

```python
import functools
import math

import numpy as np
import jax
import jax.numpy as jnp
from jax import lax
from jax.experimental import pallas as pl
from jax.experimental.pallas import tpu as pltpu

D_MODEL = 2048
DEPTH = 2
HEAD_DIM = 64
GROUP_WIDTH = 512
N_FOX_HEADS = 8
N_DIFF_HEADS = 4
N_DIL_HEADS = 8
D_FF = 4 * D_MODEL
DILATED_BRANCHES = ((128, 1), (512, 4), (2048, 16))
ALIBI_MAX_EXP = 8.0
LN_EPS = 1e-5
RMS_EPS = 1e-5
DEEPNORM_ALPHA = (2 * DEPTH) ** 0.25
QK_SCALE = HEAD_DIM ** -0.5

LANES = 128
HEADS_PER_BLOCK = LANES // HEAD_DIM
VMEM_LIMIT_BYTES = 56 * 1024 * 1024
MASKED = -1e30

_SEG = {name: 4 * i for i, name in enumerate(
    ("fq", "fk", "fv", "sq", "sk", "sv", "dq", "dk", "dv", "gq", "gk", "gv"))}

ATT_T = 256

_F32 = jnp.float32
_BF16 = jnp.bfloat16


def _params(*sem):
    return pltpu.CompilerParams(dimension_semantics=sem, vmem_limit_bytes=VMEM_LIMIT_BYTES)


def _dot(a, b):
    return jnp.dot(a, b, preferred_element_type=_F32)


def _dot_nt(a, b):
    return lax.dot_general(a, b, (((1,), (1,)), ((), ())), preferred_element_type=_F32)


FORGET_ROW0 = 3 * GROUP_WIDTH


def _in_proj_kernel(x_ref, wt_ref, o_ref, wb_ref):
    @pl.when(pl.program_id(1) == 0)
    def _():
        wb_ref[...] = wt_ref[0].astype(_BF16)

    o_ref[...] = _dot_nt(x_ref[...], wb_ref[...]).astype(o_ref.dtype)


def _in_projection(xb, wt, layer, tm=1024, tn=768):
    m, k = xb.shape
    n = wt.shape[1] - N_FOX_HEADS
    assert FORGET_ROW0 % tn == 0

    def w_rows(j, i):
        groups = j * (tn // N_FOX_HEADS) + jnp.where(j * tn >= FORGET_ROW0, 1, 0)
        return (layer, groups * N_FOX_HEADS, 0)

    return pl.pallas_call(
        _in_proj_kernel,
        grid=(n // tn, m // tm),
        in_specs=[pl.BlockSpec((tm, k), lambda j, i: (i, 0)),
                  pl.BlockSpec((pl.Element(1), pl.Element(tn), pl.Element(k)), w_rows)],
        out_specs=pl.BlockSpec((tm, tn), lambda j, i: (i, j)),
        out_shape=jax.ShapeDtypeStruct((m, n), _BF16),
        scratch_shapes=[pltpu.VMEM((tn, k), _BF16)],
        compiler_params=_params("arbitrary", "arbitrary"),
    )(xb, wt)


def _split3(x):
    hi = x.astype(_BF16)
    r = x - hi.astype(_F32)
    mid = r.astype(_BF16)
    lo = (r - mid.astype(_F32)).astype(_BF16)
    return hi, mid, lo


def _log_sigmoid_pair(z):
    log_sig = jnp.minimum(z, 0.0) - jnp.log(1.0 + jnp.exp(-jnp.abs(z)))
    return log_sig, log_sig - z


def _forget_kernel(x_ref, w_ref, b_ref, ct_ref, *, blk):
    s = x_ref.shape[1]
    z = _dot_nt(w_ref[...].astype(_BF16), x_ref[0]) + b_ref[...]
    log_f, _ = _log_sigmoid_pair(z)
    r = lax.broadcasted_iota(jnp.int32, (blk, blk), 0)
    c = lax.broadcasted_iota(jnp.int32, (blk, blk), 1)
    upto = (r <= c).astype(_BF16)
    carry = jnp.zeros((N_FOX_HEADS, 1), _F32)
    for i in range(s // blk):
        hi, mid, lo = _split3(log_f[:, i * blk:(i + 1) * blk])
        cum = _dot(hi, upto) + _dot(mid, upto) + _dot(lo, upto) + carry
        ct_ref[0, :, i * blk:(i + 1) * blk] = cum
        carry = cum[:, blk - 1:blk]


def _forget_cumsum(xb3, wt, b_fz, layer, blk=256):
    b, s, d = xb3.shape
    return pl.pallas_call(
        functools.partial(_forget_kernel, blk=blk),
        grid=(b,),
        in_specs=[pl.BlockSpec((1, s, d), lambda i: (i, 0, 0)),
                  pl.BlockSpec((None, N_FOX_HEADS, d), lambda i: (layer, FORGET_ROW0 // N_FOX_HEADS, 0)),
                  pl.BlockSpec((N_FOX_HEADS, 1), lambda i: (0, 0))],
        out_specs=pl.BlockSpec((1, N_FOX_HEADS, s), lambda i: (i, 0, 0)),
        out_shape=jax.ShapeDtypeStruct((b, N_FOX_HEADS, s), _F32),
        compiler_params=_params("arbitrary"),
    )(xb3, wt, b_fz)


def _triangle(t):
    r = lax.broadcasted_iota(jnp.int32, (t, t), 0)
    c = lax.broadcasted_iota(jnp.int32, (t, t), 1)
    return r, c


def _head_lanes(h):
    lane = lax.broadcasted_iota(jnp.int32, (1, LANES), 1)
    return (lane >= h * HEAD_DIM) & (lane < (h + 1) * HEAD_DIM)


def _one_head(q_pair, h):
    return jnp.where(_head_lanes(h), q_pair, jnp.zeros_like(q_pair))


def _causal_probs(s, causal, with_sums):
    t, n = s.shape
    diag = s[:, n - t:] if causal is None else jnp.where(causal, s[:, n - t:], MASKED)
    m = jnp.max(diag, axis=-1, keepdims=True)
    if n > t:
        left = s[:, :n - t]
        m = jnp.maximum(m, jnp.max(left, axis=-1, keepdims=True))
    p_diag = jnp.exp(diag - m)
    l = jnp.sum(p_diag, axis=-1, keepdims=True) if with_sums else None
    if n == t:
        return p_diag, l
    p_left = jnp.exp(left - m)
    if with_sums:
        l = l + jnp.sum(p_left, axis=-1, keepdims=True)
    return jnp.concatenate([p_left, p_diag], axis=-1), l


def _values_with_ones(v_pair, h):
    return jnp.where(_head_lanes(h), v_pair, jnp.ones_like(v_pair))


def _normalise(o):
    return o / pltpu.roll(o, HEAD_DIM, 1)


def _fox_kernel(q_ref, k_ref, v_ref, ct_ref, o_ref):
    t = ATT_T
    s_len = q_ref.shape[1]
    row, col = _triangle(t)
    causal = row >= col
    first = _head_lanes(0)
    values = [_values_with_ones(v_ref[0], h) for h in range(HEADS_PER_BLOCK)]
    for i in range(s_len // t):
        n = (i + 1) * t
        q_pair = q_ref[0, i * t:n, :] * QK_SCALE
        outs = []
        for h in range(HEADS_PER_BLOCK):
            c_row = ct_ref[0, h, :, :n]
            key_bias = c_row[:, n - 1:n] - c_row
            s = _dot_nt(_one_head(q_pair, h), k_ref[0, :n, :]) + key_bias
            p, _ = _causal_probs(s, causal, False)
            outs.append(_normalise(_dot(p.astype(_BF16), values[h][:n])))
        o_ref[0, i * t:n, :] = jnp.where(first, outs[0], outs[1]).astype(o_ref.dtype)


def _sb_kernel(q_ref, k_ref, v_ref, o_ref):
    t = ATT_T
    s_len = q_ref.shape[1]
    row, col = _triangle(t)
    strict = row > col
    later_ones = strict.astype(_BF16)
    first = _head_lanes(0)
    for i in range(s_len // t):
        n = (i + 1) * t
        q_pair = q_ref[0, i * t:n, :] * QK_SCALE
        outs = []
        for h in range(HEADS_PER_BLOCK):
            z = _dot_nt(_one_head(q_pair, h), k_ref[0, :n, :])
            log_beta, log_rest = _log_sigmoid_pair(z)
            tail = jnp.zeros((t, 1), _F32)
            chunks = [None] * (i + 1)
            for c in range(i, -1, -1):
                lr = log_rest[:, c * t:(c + 1) * t]
                if c == i:
                    lr = jnp.where(strict, lr, 0.0)
                hi = lr.astype(_BF16)
                lo = (lr - hi.astype(_F32)).astype(_BF16)
                later = _dot(hi, later_ones) + _dot(lo, later_ones)
                a = jnp.exp(log_beta[:, c * t:(c + 1) * t] + later + tail)
                if c == i:
                    a = jnp.where(strict, a, 0.0)
                chunks[c] = a.astype(_BF16)
                tail = tail + jnp.sum(lr, axis=-1, keepdims=True)
            a_all = chunks[0] if i == 0 else jnp.concatenate(chunks, axis=-1)
            outs.append(_dot(a_all, v_ref[0, :n, :]))
        o_ref[0, i * t:n, :] = jnp.where(first, outs[0], outs[1]).astype(o_ref.dtype)


def _diff_kernel(slopes_ref, q_ref, k_ref, v_ref, lam_ref, g_ref, o_ref, *, lam_init):
    t = ATT_T
    s_len = q_ref.shape[1]
    slope = slopes_ref[pl.program_id(1)]
    row, col = _triangle(t)
    causal = row >= col
    lq1, lk1, lq2, lk2 = (lam_ref[n:n + 1, :] for n in range(4))
    lam = (jnp.exp(jnp.sum(lq1 * lk1, axis=-1, keepdims=True))
           - jnp.exp(jnp.sum(lq2 * lk2, axis=-1, keepdims=True)) + lam_init)
    gain = g_ref[...] * (1.0 - lam_init)
    for i in range(s_len // t):
        n = (i + 1) * t
        q_pair = q_ref[0, i * t:n, :] * QK_SCALE
        key_bias = slope * (lax.broadcasted_iota(jnp.int32, (1, n), 1) - (n - 1)).astype(_F32)
        p1, l1 = _causal_probs(_dot_nt(_one_head(q_pair, 0), k_ref[0, :n, :]) + key_bias, causal, True)
        p2, l2 = _causal_probs(_dot_nt(_one_head(q_pair, 1), k_ref[0, :n, :]) + key_bias, causal, True)
        a = p1 * (1.0 / l1) - p2 * (lam / l2)
        od = _dot(a.astype(_BF16), v_ref[0, :n, :])
        od = od * lax.rsqrt(jnp.mean(od * od, axis=-1, keepdims=True) + RMS_EPS)
        o_ref[0, i * t:n, :] = (od * gain).astype(o_ref.dtype)


def _dil_kernel(slopes_ref, q_ref, k_ref, v_ref, logm_ref, ndist_ref, o_ref, bias_ref):
    t = ATT_T
    s_len = q_ref.shape[1]
    first = _head_lanes(0)
    values = [_values_with_ones(v_ref[0], h) for h in range(HEADS_PER_BLOCK)]
    for h in range(HEADS_PER_BLOCK):
        slope = slopes_ref[N_DIFF_HEADS + HEADS_PER_BLOCK * pl.program_id(1) + h]
        bias_ref[h] = logm_ref[...] + slope * ndist_ref[...]
    for i in range(s_len // t):
        n = (i + 1) * t
        q_pair = q_ref[0, i * t:n, :] * QK_SCALE
        outs = []
        for h in range(HEADS_PER_BLOCK):
            s = _dot_nt(_one_head(q_pair, h), k_ref[0, :n, :]) + bias_ref[h, :, s_len - n:]
            p, _ = _causal_probs(s, None, False)
            outs.append(_normalise(_dot(p.astype(_BF16), values[h][:n])))
        o_ref[0, i * t:n, :] = jnp.where(first, outs[0], outs[1]).astype(o_ref.dtype)


def _attention_call(kernel, h3, extra_specs, extra_args, segs, scalars=(), scratch_shapes=()):
    b, s, _ = h3.shape
    seg_q, seg_k, seg_v = segs

    def col_block(seg):
        return pl.BlockSpec((1, s, LANES), lambda bb, g, *_: (bb, 0, seg + g))

    grid_spec = pltpu.PrefetchScalarGridSpec(
        num_scalar_prefetch=len(scalars),
        grid=(b, GROUP_WIDTH // LANES),
        in_specs=[col_block(seg_q), col_block(seg_k), col_block(seg_v)] + list(extra_specs),
        out_specs=col_block(0),
        scratch_shapes=list(scratch_shapes))
    return pl.pallas_call(
        kernel,
        grid_spec=grid_spec,
        out_shape=jax.ShapeDtypeStruct((b, s, GROUP_WIDTH), _BF16),
        compiler_params=_params("arbitrary", "arbitrary"),
    )(*scalars, h3, h3, h3, *extra_args)


def _dilated_tables(s):
    r = lax.broadcasted_iota(jnp.int32, (ATT_T, s), 0)
    c = lax.broadcasted_iota(jnp.int32, (ATT_T, s), 1)
    dist = (s - ATT_T) + r - c
    mult = jnp.zeros(dist.shape, _F32)
    for window, dilation in DILATED_BRANCHES:
        hit = (dist >= 0) & (dist <= window) & (dist % dilation == 0)
        mult = mult + hit.astype(_F32)
    logm = jnp.where(mult > 0, jnp.log(jnp.maximum(mult, 1.0)), MASKED)
    return logm, -jnp.maximum(dist, 0).astype(_F32)


def _layer_norm(z, g, b):
    mu = jnp.mean(z, axis=-1, keepdims=True)
    zc = z - mu
    var = jnp.mean(zc * zc, axis=-1, keepdims=True)
    return zc * lax.rsqrt(var + LN_EPS) * g + b


def _out_proj_kernel(m0_ref, m1_ref, m2_ref, m3_ref, w_ref, x_ref, g_ref, b_ref, xo_ref, xob_ref):
    y = None
    for n, m_ref in enumerate((m0_ref, m1_ref, m2_ref, m3_ref)):
        part = _dot(m_ref[...], w_ref[n * GROUP_WIDTH:(n + 1) * GROUP_WIDTH, :])
        y = part if y is None else y + part
    out = _layer_norm(DEEPNORM_ALPHA * x_ref[...] + y, g_ref[...], b_ref[...])
    xo_ref[...] = out
    xob_ref[...] = out.astype(_BF16)


def _out_projection(mixed, w, x, g, b, layer, tm=256):
    m, d = x.shape
    row = lambda i: (i, 0)
    const = lambda i: (0, 0)
    return pl.pallas_call(
        _out_proj_kernel,
        grid=(m // tm,),
        in_specs=[pl.BlockSpec((tm, GROUP_WIDTH), row)] * 4 + [
            pl.BlockSpec((None, d, d), lambda i: (layer, 0, 0)), pl.BlockSpec((tm, d), row),
            pl.BlockSpec((1, d), const), pl.BlockSpec((1, d), const)],
        out_specs=[pl.BlockSpec((tm, d), row), pl.BlockSpec((tm, d), row)],
        out_shape=[jax.ShapeDtypeStruct((m, d), _F32), jax.ShapeDtypeStruct((m, d), _BF16)],
        compiler_params=_params("arbitrary"),
    )(*mixed, w, x, g, b)


def _mlp_kernel(xb_ref, x_ref, w1_ref, w2_ref, g_ref, b_ref, xo_ref, xob_ref, acc_ref):
    f = pl.program_id(1)

    @pl.when(f == 0)
    def _():
        acc_ref[...] = jnp.zeros_like(acc_ref)

    a = jnp.maximum(_dot(xb_ref[...], w1_ref[...]), 0.0)
    acc_ref[...] += _dot((a * a).astype(_BF16), w2_ref[...])

    @pl.when(f == pl.num_programs(1) - 1)
    def _():
        out = _layer_norm(DEEPNORM_ALPHA * x_ref[...] + acc_ref[...], g_ref[...], b_ref[...])
        xo_ref[...] = out
        xob_ref[...] = out.astype(_BF16)


def _mlp(xb, x, w1, w2, g, b, layer, tm=512, tf=1024):
    m, d = x.shape
    ff = w1.shape[2]
    row = lambda i, f: (i, 0)
    const = lambda i, f: (0, 0)
    return pl.pallas_call(
        _mlp_kernel,
        grid=(m // tm, ff // tf),
        in_specs=[pl.BlockSpec((tm, d), row), pl.BlockSpec((tm, d), row),
                  pl.BlockSpec((None, d, tf), lambda i, f: (layer, 0, f)),
                  pl.BlockSpec((None, tf, d), lambda i, f: (layer, f, 0)),
                  pl.BlockSpec((1, d), const), pl.BlockSpec((1, d), const)],
        out_specs=[pl.BlockSpec((tm, d), row), pl.BlockSpec((tm, d), row)],
        out_shape=[jax.ShapeDtypeStruct((m, d), _F32), jax.ShapeDtypeStruct((m, d), _BF16)],
        scratch_shapes=[pltpu.VMEM((tm, d), _F32)],
        compiler_params=_params("arbitrary", "arbitrary"),
    )(xb, x, w1, w2, g, b)


def _alibi_slopes():
    n = N_DIFF_HEADS + N_DIL_HEADS
    return jnp.asarray(np.exp2(-ALIBI_MAX_EXP * np.arange(1, n + 1) / n), _F32)


def _layer(x, xb, layer_idx, wt_in, b_fz, lam_vecs, subln_g, w_out, ln1_g, ln1_b,
           w1, w2, ln2_g, ln2_b, slopes, dil_tables, batch):
    m, d = x.shape
    s = m // batch
    h3 = _in_projection(xb, wt_in, layer_idx).reshape(batch, s, -1)

    ct = _forget_cumsum(xb.reshape(batch, s, d), wt_in, b_fz, layer_idx)
    ct = ct.reshape(batch, N_FOX_HEADS, 1, s)
    o_fox = _attention_call(
        _fox_kernel, h3,
        [pl.BlockSpec((1, HEADS_PER_BLOCK, 1, s), lambda b, g: (b, g, 0, 0))],
        (ct,), (_SEG["fq"], _SEG["fk"], _SEG["fv"]))

    o_sb = _attention_call(_sb_kernel, h3, [], (), (_SEG["sq"], _SEG["sk"], _SEG["sv"]))

    lam_init = 0.8 - 0.6 * math.exp(-0.3 * layer_idx)
    o_diff = _attention_call(
        functools.partial(_diff_kernel, lam_init=lam_init), h3,
        [pl.BlockSpec((4, HEAD_DIM), lambda b, g, *_: (0, 0)),
         pl.BlockSpec((1, 2 * HEAD_DIM), lambda b, g, *_: (0, 0))],
        (lam_vecs, subln_g), (_SEG["dq"], _SEG["dk"], _SEG["dv"]), scalars=(slopes,))

    o_dil = _attention_call(
        _dil_kernel, h3,
        [pl.BlockSpec((ATT_T, s), lambda b, g, *_: (0, 0))] * 2,
        dil_tables, (_SEG["gq"], _SEG["gk"], _SEG["gv"]), scalars=(slopes,),
        scratch_shapes=[pltpu.VMEM((HEADS_PER_BLOCK, ATT_T, s), _F32)])

    mixed = [o.reshape(m, GROUP_WIDTH) for o in (o_fox, o_sb, o_diff, o_dil)]
    x, xb = _out_projection(mixed, w_out, x, ln1_g, ln1_b, layer_idx)
    return _mlp(xb, x, w1, w2, ln2_g, ln2_b, layer_idx)


def kernel(x, w_in, fox_forget_bias, diff_lambda_q1, diff_lambda_k1, diff_lambda_q2, diff_lambda_k2,
           diff_subln_gain, w_out, ln1_gain, ln1_bias, w_mlp_in, w_mlp_out, ln2_gain, ln2_bias):
    batch, s, d = x.shape
    assert d == D_MODEL and s % ATT_T == 0
    slopes = _alibi_slopes()
    dil_tables = _dilated_tables(s)
    xf = x.reshape(batch * s, d)
    xb = xf.astype(_BF16)
    wt_in = jnp.transpose(w_in, (0, 2, 1))
    w_out, w_mlp_in, w_mlp_out = (w.astype(_BF16) for w in (w_out, w_mlp_in, w_mlp_out))
    for l in range(DEPTH):
        b_fz = fox_forget_bias[l].reshape(N_FOX_HEADS, 1)
        lam_vecs = jnp.stack([diff_lambda_q1[l], diff_lambda_k1[l], diff_lambda_q2[l], diff_lambda_k2[l]])
        xf, xb = _layer(
            xf, xb, l, wt_in, b_fz, lam_vecs, diff_subln_gain[l].reshape(1, -1),
            w_out, ln1_gain[l].reshape(1, d), ln1_bias[l].reshape(1, d),
            w_mlp_in, w_mlp_out,
            ln2_gain[l].reshape(1, d), ln2_bias[l].reshape(1, d), slopes, dil_tables, batch)
    return xf.reshape(batch, s, d)
```

```python
import functools
import math

import numpy as np
import jax
import jax.numpy as jnp
from jax import lax
from jax.experimental import pallas as pl
from jax.experimental.pallas import tpu as pltpu

D_MODEL = 2048
DEPTH = 2
HEAD_DIM = 64
GROUP_WIDTH = 512
N_FOX_HEADS = 8
N_DIFF_HEADS = 4
N_DIL_HEADS = 8
D_FF = 4 * D_MODEL
DILATED_BRANCHES = ((128, 1), (512, 4), (2048, 16))
ALIBI_MAX_EXP = 8.0
LN_EPS = 1e-5
RMS_EPS = 1e-5
DEEPNORM_ALPHA = (2 * DEPTH) ** 0.25
QK_SCALE = HEAD_DIM ** -0.5

LANES = 128
HEADS_PER_BLOCK = LANES // HEAD_DIM
VMEM_LIMIT_BYTES = 56 * 1024 * 1024
MASKED = -1e30

_SEG = {name: 4 * i for i, name in enumerate(
    ("fq", "fk", "fv", "sq", "sk", "sv", "dq", "dk", "dv", "gq", "gk", "gv"))}

ATT_T = 256

_F32 = jnp.float32
_BF16 = jnp.bfloat16


def _params(*sem):
    return pltpu.CompilerParams(dimension_semantics=sem, vmem_limit_bytes=VMEM_LIMIT_BYTES)


def _dot(a, b):
    return jnp.dot(a, b, preferred_element_type=_F32)


def _dot_nt(a, b):
    return lax.dot_general(a, b, (((1,), (1,)), ((), ())), preferred_element_type=_F32)


FORGET_ROW0 = 3 * GROUP_WIDTH


def _in_proj_kernel(x_ref, wt_ref, o_ref, wb_ref):
    @pl.when(pl.program_id(1) == 0)
    def _():
        wb_ref[...] = wt_ref[0].astype(_BF16)

    o_ref[...] = _dot_nt(x_ref[...], wb_ref[...]).astype(o_ref.dtype)


def _in_projection(xb, wt, layer, tm=1024, tn=768):
    m, k = xb.shape
    n = wt.shape[1] - N_FOX_HEADS
    assert FORGET_ROW0 % tn == 0

    def w_rows(j, i):
        groups = j * (tn // N_FOX_HEADS) + jnp.where(j * tn >= FORGET_ROW0, 1, 0)
        return (layer, groups * N_FOX_HEADS, 0)

    return pl.pallas_call(
        _in_proj_kernel,
        grid=(n // tn, m // tm),
        in_specs=[pl.BlockSpec((tm, k), lambda j, i: (i, 0)),
                  pl.BlockSpec((pl.Element(1), pl.Element(tn), pl.Element(k)), w_rows)],
        out_specs=pl.BlockSpec((tm, tn), lambda j, i: (i, j)),
        out_shape=jax.ShapeDtypeStruct((m, n), _BF16),
        scratch_shapes=[pltpu.VMEM((tn, k), _BF16)],
        compiler_params=_params("arbitrary", "arbitrary"),
    )(xb, wt)


def _split3(x):
    hi = x.astype(_BF16)
    r = x - hi.astype(_F32)
    mid = r.astype(_BF16)
    lo = (r - mid.astype(_F32)).astype(_BF16)
    return hi, mid, lo


def _log_sigmoid_pair(z):
    log_sig = jnp.minimum(z, 0.0) - jnp.log(1.0 + jnp.exp(-jnp.abs(z)))
    return log_sig, log_sig - z


def _forget_kernel(x_ref, w_ref, b_ref, ct_ref, *, blk):
    s = x_ref.shape[1]
    z = _dot_nt(w_ref[...].astype(_BF16), x_ref[0]) + b_ref[...]
    log_f, _ = _log_sigmoid_pair(z)
    r = lax.broadcasted_iota(jnp.int32, (blk, blk), 0)
    c = lax.broadcasted_iota(jnp.int32, (blk, blk), 1)
    upto = (r <= c).astype(_BF16)
    carry = jnp.zeros((N_FOX_HEADS, 1), _F32)
    for i in range(s // blk):
        hi, mid, lo = _split3(log_f[:, i * blk:(i + 1) * blk])
        cum = _dot(hi, upto) + _dot(mid, upto) + _dot(lo, upto) + carry
        ct_ref[0, :, i * blk:(i + 1) * blk] = cum
        carry = cum[:, blk - 1:blk]


def _forget_cumsum(xb3, wt, b_fz, layer, blk=256):
    b, s, d = xb3.shape
    return pl.pallas_call(
        functools.partial(_forget_kernel, blk=blk),
        grid=(b,),
        in_specs=[pl.BlockSpec((1, s, d), lambda i: (i, 0, 0)),
                  pl.BlockSpec((None, N_FOX_HEADS, d), lambda i: (layer, FORGET_ROW0 // N_FOX_HEADS, 0)),
                  pl.BlockSpec((N_FOX_HEADS, 1), lambda i: (0, 0))],
        out_specs=pl.BlockSpec((1, N_FOX_HEADS, s), lambda i: (i, 0, 0)),
        out_shape=jax.ShapeDtypeStruct((b, N_FOX_HEADS, s), _F32),
        compiler_params=_params("arbitrary"),
    )(xb3, wt, b_fz)


def _triangle(t):
    r = lax.broadcasted_iota(jnp.int32, (t, t), 0)
    c = lax.broadcasted_iota(jnp.int32, (t, t), 1)
    return r, c


def _head_lanes(h):
    lane = lax.broadcasted_iota(jnp.int32, (1, LANES), 1)
    return (lane >= h * HEAD_DIM) & (lane < (h + 1) * HEAD_DIM)


def _one_head(q_pair, h):
    return jnp.where(_head_lanes(h), q_pair, jnp.zeros_like(q_pair))


def _causal_probs(s, causal, with_sums):
    t, n = s.shape
    diag = s[:, n - t:] if causal is None else jnp.where(causal, s[:, n - t:], MASKED)
    m = jnp.max(diag, axis=-1, keepdims=True)
    if n > t:
        left = s[:, :n - t]
        m = jnp.maximum(m, jnp.max(left, axis=-1, keepdims=True))
    p_diag = jnp.exp(diag - m)
    l = jnp.sum(p_diag, axis=-1, keepdims=True) if with_sums else None
    if n == t:
        return p_diag, l
    p_left = jnp.exp(left - m)
    if with_sums:
        l = l + jnp.sum(p_left, axis=-1, keepdims=True)
    return jnp.concatenate([p_left, p_diag], axis=-1), l


def _values_with_ones(v_pair, h):
    return jnp.where(_head_lanes(h), v_pair, jnp.ones_like(v_pair))


def _normalise(o):
    return o / pltpu.roll(o, HEAD_DIM, 1)


def _pipelined(units, scores, finish):
    pending = scores(units[0])
    for u, unit in enumerate(units):
        ahead = scores(units[u + 1]) if u + 1 < len(units) else None
        finish(unit, pending)
        pending = ahead


def _fox_kernel(q_ref, k_ref, v_ref, ct_ref, o_ref):
    t = ATT_T
    s_len = q_ref.shape[1]
    row, col = _triangle(t)
    causal = row >= col
    first = _head_lanes(0)
    values = [_values_with_ones(v_ref[0], h) for h in range(HEADS_PER_BLOCK)]

    def scores(unit):
        i, h = unit
        n = (i + 1) * t
        c_row = ct_ref[0, h, :, :n]
        key_bias = c_row[:, n - 1:n] - c_row
        q_pair = q_ref[0, i * t:n, :] * QK_SCALE
        return _dot_nt(_one_head(q_pair, h), k_ref[0, :n, :]) + key_bias

    outs = {}

    def finish(unit, s):
        i, h = unit
        n = (i + 1) * t
        p, _ = _causal_probs(s, causal, False)
        outs[h] = _normalise(_dot(p.astype(_BF16), values[h][:n]))
        if h == HEADS_PER_BLOCK - 1:
            o_ref[0, i * t:n, :] = jnp.where(first, outs[0], outs[1]).astype(o_ref.dtype)

    _pipelined([(i, h) for i in range(s_len // t) for h in range(HEADS_PER_BLOCK)], scores, finish)


def _sb_kernel(q_ref, k_ref, v_ref, o_ref):
    t = ATT_T
    s_len = q_ref.shape[1]
    row, col = _triangle(t)
    strict = row > col
    later_ones = strict.astype(_BF16)
    later_ones2 = jnp.concatenate([later_ones, later_ones], axis=0)
    first = _head_lanes(0)

    def scores(unit):
        i, h = unit
        q_pair = q_ref[0, i * t:(i + 1) * t, :] * QK_SCALE
        return _dot_nt(_one_head(q_pair, h), k_ref[0, :(i + 1) * t, :])

    outs = {}

    def finish(unit, z):
        i, h = unit
        n = (i + 1) * t
        log_beta, log_rest = _log_sigmoid_pair(z)
        tail = jnp.zeros((t, 1), _F32)
        chunks = [None] * (i + 1)
        for c in range(i, -1, -1):
            lr = log_rest[:, c * t:(c + 1) * t]
            if c == i:
                lr = jnp.where(strict, lr, 0.0)
            hi = lr.astype(_BF16)
            lo = (lr - hi.astype(_F32)).astype(_BF16)
            later = _dot(jnp.concatenate([hi, lo], axis=-1), later_ones2)
            a = jnp.exp(log_beta[:, c * t:(c + 1) * t] + later + tail)
            if c == i:
                a = jnp.where(strict, a, 0.0)
            chunks[c] = a.astype(_BF16)
            tail = tail + jnp.sum(lr, axis=-1, keepdims=True)
        a_all = chunks[0] if i == 0 else jnp.concatenate(chunks, axis=-1)
        outs[h] = _dot(a_all, v_ref[0, :n, :])
        if h == HEADS_PER_BLOCK - 1:
            o_ref[0, i * t:n, :] = jnp.where(first, outs[0], outs[1]).astype(o_ref.dtype)

    _pipelined([(i, h) for i in range(s_len // t) for h in range(HEADS_PER_BLOCK)], scores, finish)


def _diff_kernel(slopes_ref, q_ref, k_ref, v_ref, lam_ref, g_ref, o_ref, *, lam_init):
    t = ATT_T
    s_len = q_ref.shape[1]
    slope = slopes_ref[pl.program_id(1)]
    row, col = _triangle(t)
    causal = row >= col
    lq1, lk1, lq2, lk2 = (lam_ref[n:n + 1, :] for n in range(4))
    lam = (jnp.exp(jnp.sum(lq1 * lk1, axis=-1, keepdims=True))
           - jnp.exp(jnp.sum(lq2 * lk2, axis=-1, keepdims=True)) + lam_init)
    gain = g_ref[...] * (1.0 - lam_init)

    def scores(unit):
        i, h = unit
        n = (i + 1) * t
        q_pair = q_ref[0, i * t:n, :] * QK_SCALE
        key_bias = slope * (lax.broadcasted_iota(jnp.int32, (1, n), 1) - (n - 1)).astype(_F32)
        return _dot_nt(_one_head(q_pair, h), k_ref[0, :n, :]) + key_bias

    maps = {}

    def finish(unit, s):
        i, h = unit
        n = (i + 1) * t
        maps[h] = _causal_probs(s, causal, True)
        if h == 1:
            (p1, l1), (p2, l2) = maps[0], maps[1]
            a = p1 * (1.0 / l1) - p2 * (lam / l2)
            od = _dot(a.astype(_BF16), v_ref[0, :n, :])
            od = od * lax.rsqrt(jnp.mean(od * od, axis=-1, keepdims=True) + RMS_EPS)
            o_ref[0, i * t:n, :] = (od * gain).astype(o_ref.dtype)

    _pipelined([(i, h) for i in range(s_len // t) for h in range(2)], scores, finish)


def _dil_kernel(slopes_ref, q_ref, k_ref, v_ref, logm_ref, ndist_ref, o_ref, bias_ref):
    t = ATT_T
    s_len = q_ref.shape[1]
    first = _head_lanes(0)
    values = [_values_with_ones(v_ref[0], h) for h in range(HEADS_PER_BLOCK)]
    for h in range(HEADS_PER_BLOCK):
        slope = slopes_ref[N_DIFF_HEADS + HEADS_PER_BLOCK * pl.program_id(1) + h]
        bias_ref[h] = logm_ref[...] + slope * ndist_ref[...]

    def scores(unit):
        i, h = unit
        n = (i + 1) * t
        q_pair = q_ref[0, i * t:n, :] * QK_SCALE
        return _dot_nt(_one_head(q_pair, h), k_ref[0, :n, :]) + bias_ref[h, :, s_len - n:]

    outs = {}

    def finish(unit, s):
        i, h = unit
        n = (i + 1) * t
        p, _ = _causal_probs(s, None, False)
        outs[h] = _normalise(_dot(p.astype(_BF16), values[h][:n]))
        if h == HEADS_PER_BLOCK - 1:
            o_ref[0, i * t:n, :] = jnp.where(first, outs[0], outs[1]).astype(o_ref.dtype)

    _pipelined([(i, h) for i in range(s_len // t) for h in range(HEADS_PER_BLOCK)], scores, finish)


def _attention_call(kernel, h3, extra_specs, extra_args, segs, scalars=(), scratch_shapes=()):
    b, s, _ = h3.shape
    seg_q, seg_k, seg_v = segs

    def col_block(seg):
        return pl.BlockSpec((1, s, LANES), lambda bb, g, *_: (bb, 0, seg + g))

    grid_spec = pltpu.PrefetchScalarGridSpec(
        num_scalar_prefetch=len(scalars),
        grid=(b, GROUP_WIDTH // LANES),
        in_specs=[col_block(seg_q), col_block(seg_k), col_block(seg_v)] + list(extra_specs),
        out_specs=col_block(0),
        scratch_shapes=list(scratch_shapes))
    return pl.pallas_call(
        kernel,
        grid_spec=grid_spec,
        out_shape=jax.ShapeDtypeStruct((b, s, GROUP_WIDTH), _BF16),
        compiler_params=_params("arbitrary", "arbitrary"),
    )(*scalars, h3, h3, h3, *extra_args)


def _dilated_tables(s):
    r = lax.broadcasted_iota(jnp.int32, (ATT_T, s), 0)
    c = lax.broadcasted_iota(jnp.int32, (ATT_T, s), 1)
    dist = (s - ATT_T) + r - c
    mult = jnp.zeros(dist.shape, _F32)
    for window, dilation in DILATED_BRANCHES:
        hit = (dist >= 0) & (dist <= window) & (dist % dilation == 0)
        mult = mult + hit.astype(_F32)
    logm = jnp.where(mult > 0, jnp.log(jnp.maximum(mult, 1.0)), MASKED)
    return logm, -jnp.maximum(dist, 0).astype(_F32)


def _layer_norm(z, g, b):
    mu = jnp.mean(z, axis=-1, keepdims=True)
    zc = z - mu
    var = jnp.mean(zc * zc, axis=-1, keepdims=True)
    return zc * lax.rsqrt(var + LN_EPS) * g + b


def _out_proj_kernel(m0_ref, m1_ref, m2_ref, m3_ref, w_ref, x_ref, g_ref, b_ref, xo_ref, xob_ref):
    y = None
    for n, m_ref in enumerate((m0_ref, m1_ref, m2_ref, m3_ref)):
        part = _dot(m_ref[...], w_ref[n * GROUP_WIDTH:(n + 1) * GROUP_WIDTH, :])
        y = part if y is None else y + part
    out = _layer_norm(DEEPNORM_ALPHA * x_ref[...] + y, g_ref[...], b_ref[...])
    xo_ref[...] = out
    xob_ref[...] = out.astype(_BF16)


def _out_projection(mixed, w, x, g, b, layer, tm=256):
    m, d = x.shape
    row = lambda i: (i, 0)
    const = lambda i: (0, 0)
    return pl.pallas_call(
        _out_proj_kernel,
        grid=(m // tm,),
        in_specs=[pl.BlockSpec((tm, GROUP_WIDTH), row)] * 4 + [
            pl.BlockSpec((None, d, d), lambda i: (layer, 0, 0)), pl.BlockSpec((tm, d), row),
            pl.BlockSpec((1, d), const), pl.BlockSpec((1, d), const)],
        out_specs=[pl.BlockSpec((tm, d), row), pl.BlockSpec((tm, d), row)],
        out_shape=[jax.ShapeDtypeStruct((m, d), _F32), jax.ShapeDtypeStruct((m, d), _BF16)],
        compiler_params=_params("arbitrary"),
    )(*mixed, w, x, g, b)


def _mlp_kernel(xb_ref, x_ref, w1_ref, w2_ref, g_ref, b_ref, xo_ref, xob_ref, acc_ref):
    f = pl.program_id(1)

    @pl.when(f == 0)
    def _():
        acc_ref[...] = jnp.zeros_like(acc_ref)

    a = jnp.maximum(_dot(xb_ref[...], w1_ref[...]), 0.0)
    acc_ref[...] += _dot((a * a).astype(_BF16), w2_ref[...])

    @pl.when(f == pl.num_programs(1) - 1)
    def _():
        out = _layer_norm(DEEPNORM_ALPHA * x_ref[...] + acc_ref[...], g_ref[...], b_ref[...])
        xo_ref[...] = out
        xob_ref[...] = out.astype(_BF16)


def _mlp(xb, x, w1, w2, g, b, layer, tm=512, tf=1024):
    m, d = x.shape
    ff = w1.shape[2]
    row = lambda i, f: (i, 0)
    const = lambda i, f: (0, 0)
    return pl.pallas_call(
        _mlp_kernel,
        grid=(m // tm, ff // tf),
        in_specs=[pl.BlockSpec((tm, d), row), pl.BlockSpec((tm, d), row),
                  pl.BlockSpec((None, d, tf), lambda i, f: (layer, 0, f)),
                  pl.BlockSpec((None, tf, d), lambda i, f: (layer, f, 0)),
                  pl.BlockSpec((1, d), const), pl.BlockSpec((1, d), const)],
        out_specs=[pl.BlockSpec((tm, d), row), pl.BlockSpec((tm, d), row)],
        out_shape=[jax.ShapeDtypeStruct((m, d), _F32), jax.ShapeDtypeStruct((m, d), _BF16)],
        scratch_shapes=[pltpu.VMEM((tm, d), _F32)],
        compiler_params=_params("arbitrary", "arbitrary"),
    )(xb, x, w1, w2, g, b)


def _alibi_slopes():
    n = N_DIFF_HEADS + N_DIL_HEADS
    return jnp.asarray(np.exp2(-ALIBI_MAX_EXP * np.arange(1, n + 1) / n), _F32)


def _layer(x, xb, layer_idx, wt_in, b_fz, lam_vecs, subln_g, w_out, ln1_g, ln1_b,
           w1, w2, ln2_g, ln2_b, slopes, dil_tables, batch):
    m, d = x.shape
    s = m // batch
    h3 = _in_projection(xb, wt_in, layer_idx).reshape(batch, s, -1)

    ct = _forget_cumsum(xb.reshape(batch, s, d), wt_in, b_fz, layer_idx)
    ct = ct.reshape(batch, N_FOX_HEADS, 1, s)
    o_fox = _attention_call(
        _fox_kernel, h3,
        [pl.BlockSpec((1, HEADS_PER_BLOCK, 1, s), lambda b, g: (b, g, 0, 0))],
        (ct,), (_SEG["fq"], _SEG["fk"], _SEG["fv"]))

    o_sb = _attention_call(_sb_kernel, h3, [], (), (_SEG["sq"], _SEG["sk"], _SEG["sv"]))

    lam_init = 0.8 - 0.6 * math.exp(-0.3 * layer_idx)
    o_diff = _attention_call(
        functools.partial(_diff_kernel, lam_init=lam_init), h3,
        [pl.BlockSpec((4, HEAD_DIM), lambda b, g, *_: (0, 0)),
         pl.BlockSpec((1, 2 * HEAD_DIM), lambda b, g, *_: (0, 0))],
        (lam_vecs, subln_g), (_SEG["dq"], _SEG["dk"], _SEG["dv"]), scalars=(slopes,))

    o_dil = _attention_call(
        _dil_kernel, h3,
        [pl.BlockSpec((ATT_T, s), lambda b, g, *_: (0, 0))] * 2,
        dil_tables, (_SEG["gq"], _SEG["gk"], _SEG["gv"]), scalars=(slopes,),
        scratch_shapes=[pltpu.VMEM((HEADS_PER_BLOCK, ATT_T, s), _F32)])

    mixed = [o.reshape(m, GROUP_WIDTH) for o in (o_fox, o_sb, o_diff, o_dil)]
    x, xb = _out_projection(mixed, w_out, x, ln1_g, ln1_b, layer_idx)
    return _mlp(xb, x, w1, w2, ln2_g, ln2_b, layer_idx)


def kernel(x, w_in, fox_forget_bias, diff_lambda_q1, diff_lambda_k1, diff_lambda_q2, diff_lambda_k2,
           diff_subln_gain, w_out, ln1_gain, ln1_bias, w_mlp_in, w_mlp_out, ln2_gain, ln2_bias):
    batch, s, d = x.shape
    assert d == D_MODEL and s % ATT_T == 0
    slopes = _alibi_slopes()
    dil_tables = _dilated_tables(s)
    xf = x.reshape(batch * s, d)
    xb = xf.astype(_BF16)
    wt_in = jnp.transpose(w_in, (0, 2, 1))
    w_out, w_mlp_in, w_mlp_out = (w.astype(_BF16) for w in (w_out, w_mlp_in, w_mlp_out))
    for l in range(DEPTH):
        b_fz = fox_forget_bias[l].reshape(N_FOX_HEADS, 1)
        lam_vecs = jnp.stack([diff_lambda_q1[l], diff_lambda_k1[l], diff_lambda_q2[l], diff_lambda_k2[l]])
        xf, xb = _layer(
            xf, xb, l, wt_in, b_fz, lam_vecs, diff_subln_gain[l].reshape(1, -1),
            w_out, ln1_gain[l].reshape(1, d), ln1_bias[l].reshape(1, d),
            w_mlp_in, w_mlp_out,
            ln2_gain[l].reshape(1, d), ln2_bias[l].reshape(1, d), slopes, dil_tables, batch)
    return xf.reshape(batch, s, d)
```

```python
import functools
import math

import numpy as np
import jax
import jax.numpy as jnp
from jax import lax
from jax.experimental import pallas as pl
from jax.experimental.pallas import tpu as pltpu

D_MODEL = 2048
DEPTH = 2
HEAD_DIM = 64
GROUP_WIDTH = 512
N_FOX_HEADS = 8
N_DIFF_HEADS = 4
N_DIL_HEADS = 8
D_FF = 4 * D_MODEL
DILATED_BRANCHES = ((128, 1), (512, 4), (2048, 16))
ALIBI_MAX_EXP = 8.0
LN_EPS = 1e-5
RMS_EPS = 1e-5
DEEPNORM_ALPHA = (2 * DEPTH) ** 0.25
QK_SCALE = HEAD_DIM ** -0.5

LANES = 128
HEADS_PER_BLOCK = LANES // HEAD_DIM
VMEM_LIMIT_BYTES = 56 * 1024 * 1024
MASKED = -1e30

_SEG = {name: 4 * i for i, name in enumerate(
    ("fq", "fk", "fv", "sq", "sk", "sv", "dq", "dk", "dv", "gq", "gk", "gv"))}

ATT_T = 256

_F32 = jnp.float32
_BF16 = jnp.bfloat16


def _params(*sem):
    return pltpu.CompilerParams(dimension_semantics=sem, vmem_limit_bytes=VMEM_LIMIT_BYTES)


def _dot(a, b):
    return jnp.dot(a, b, preferred_element_type=_F32)


def _dot_nt(a, b):
    return lax.dot_general(a, b, (((1,), (1,)), ((), ())), preferred_element_type=_F32)


FORGET_ROW0 = 3 * GROUP_WIDTH


def _in_proj_kernel(x_ref, wt_ref, o_ref, wb_ref):
    @pl.when(pl.program_id(1) == 0)
    def _():
        wb_ref[...] = wt_ref[0].astype(_BF16)

    o_ref[...] = _dot_nt(x_ref[...], wb_ref[...]).astype(o_ref.dtype)


def _in_projection(xb, wt, layer, tm=2048, tn=768):
    m, k = xb.shape
    n = wt.shape[1] - N_FOX_HEADS
    assert FORGET_ROW0 % tn == 0

    def w_rows(j, i):
        groups = j * (tn // N_FOX_HEADS) + jnp.where(j * tn >= FORGET_ROW0, 1, 0)
        return (layer, groups * N_FOX_HEADS, 0)

    return pl.pallas_call(
        _in_proj_kernel,
        grid=(n // tn, m // tm),
        in_specs=[pl.BlockSpec((tm, k), lambda j, i: (i, 0)),
                  pl.BlockSpec((pl.Element(1), pl.Element(tn), pl.Element(k)), w_rows)],
        out_specs=pl.BlockSpec((tm, tn), lambda j, i: (i, j)),
        out_shape=jax.ShapeDtypeStruct((m, n), _BF16),
        scratch_shapes=[pltpu.VMEM((tn, k), _BF16)],
        compiler_params=_params("arbitrary", "arbitrary"),
    )(xb, wt)


def _split3(x):
    hi = x.astype(_BF16)
    r = x - hi.astype(_F32)
    mid = r.astype(_BF16)
    lo = (r - mid.astype(_F32)).astype(_BF16)
    return hi, mid, lo


def _log_sigmoid_pair(z):
    log_sig = jnp.minimum(z, 0.0) - jnp.log(1.0 + jnp.exp(-jnp.abs(z)))
    return log_sig, log_sig - z


def _forget_kernel(x_ref, w_ref, b_ref, ct_ref, *, blk):
    s = x_ref.shape[1]
    z = _dot_nt(w_ref[...].astype(_BF16), x_ref[0]) + b_ref[...]
    log_f, _ = _log_sigmoid_pair(z)
    r = lax.broadcasted_iota(jnp.int32, (blk, blk), 0)
    c = lax.broadcasted_iota(jnp.int32, (blk, blk), 1)
    upto = (r <= c).astype(_BF16)
    carry = jnp.zeros((N_FOX_HEADS, 1), _F32)
    for i in range(s // blk):
        hi, mid, lo = _split3(log_f[:, i * blk:(i + 1) * blk])
        cum = _dot(hi, upto) + _dot(mid, upto) + _dot(lo, upto) + carry
        ct_ref[0, :, i * blk:(i + 1) * blk] = cum
        carry = cum[:, blk - 1:blk]


def _forget_cumsum(xb3, wt, b_fz, layer, blk=256):
    b, s, d = xb3.shape
    return pl.pallas_call(
        functools.partial(_forget_kernel, blk=blk),
        grid=(b,),
        in_specs=[pl.BlockSpec((1, s, d), lambda i: (i, 0, 0)),
                  pl.BlockSpec((None, N_FOX_HEADS, d), lambda i: (layer, FORGET_ROW0 // N_FOX_HEADS, 0)),
                  pl.BlockSpec((N_FOX_HEADS, 1), lambda i: (0, 0))],
        out_specs=pl.BlockSpec((1, N_FOX_HEADS, s), lambda i: (i, 0, 0)),
        out_shape=jax.ShapeDtypeStruct((b, N_FOX_HEADS, s), _F32),
        compiler_params=_params("arbitrary"),
    )(xb3, wt, b_fz)


def _triangle(t):
    r = lax.broadcasted_iota(jnp.int32, (t, t), 0)
    c = lax.broadcasted_iota(jnp.int32, (t, t), 1)
    return r, c


def _head_lanes(h):
    lane = lax.broadcasted_iota(jnp.int32, (1, LANES), 1)
    return (lane >= h * HEAD_DIM) & (lane < (h + 1) * HEAD_DIM)


def _one_head(q_pair, h):
    return jnp.where(_head_lanes(h), q_pair, jnp.zeros_like(q_pair))


def _causal_probs(s, causal, with_sums):
    t, n = s.shape
    diag = s[:, n - t:] if causal is None else jnp.where(causal, s[:, n - t:], MASKED)
    m = jnp.max(diag, axis=-1, keepdims=True)
    if n > t:
        left = s[:, :n - t]
        m = jnp.maximum(m, jnp.max(left, axis=-1, keepdims=True))
    p_diag = jnp.exp(diag - m)
    l = jnp.sum(p_diag, axis=-1, keepdims=True) if with_sums else None
    if n == t:
        return p_diag, l
    p_left = jnp.exp(left - m)
    if with_sums:
        l = l + jnp.sum(p_left, axis=-1, keepdims=True)
    return jnp.concatenate([p_left, p_diag], axis=-1), l


def _values_with_ones(v_pair, h):
    return jnp.where(_head_lanes(h), v_pair, jnp.ones_like(v_pair))


def _normalise(o):
    return o / pltpu.roll(o, HEAD_DIM, 1)


def _pipelined(units, scores, finish):
    pending = scores(units[0])
    for u, unit in enumerate(units):
        ahead = scores(units[u + 1]) if u + 1 < len(units) else None
        finish(unit, pending)
        pending = ahead


def _fox_kernel(q_ref, k_ref, v_ref, ct_ref, o_ref):
    t = ATT_T
    s_len = q_ref.shape[1]
    row, col = _triangle(t)
    causal = row >= col
    first = _head_lanes(0)
    values = [_values_with_ones(v_ref[0], h) for h in range(HEADS_PER_BLOCK)]

    def scores(unit):
        i, h = unit
        n = (i + 1) * t
        c_row = ct_ref[0, h, :, :n]
        key_bias = c_row[:, n - 1:n] - c_row
        q_pair = q_ref[0, i * t:n, :] * QK_SCALE
        return _dot_nt(_one_head(q_pair, h), k_ref[0, :n, :]) + key_bias

    outs = {}

    def finish(unit, s):
        i, h = unit
        n = (i + 1) * t
        p, _ = _causal_probs(s, causal, False)
        outs[h] = _normalise(_dot(p.astype(_BF16), values[h][:n]))
        if h == HEADS_PER_BLOCK - 1:
            o_ref[0, i * t:n, :] = jnp.where(first, outs[0], outs[1]).astype(o_ref.dtype)

    _pipelined([(i, h) for i in range(s_len // t) for h in range(HEADS_PER_BLOCK)], scores, finish)


def _sb_kernel(q_ref, k_ref, v_ref, o_ref):
    t = ATT_T
    s_len = q_ref.shape[1]
    row, col = _triangle(t)
    strict = row > col
    later_ones = strict.astype(_BF16)
    later_ones2 = jnp.concatenate([later_ones, later_ones], axis=0)
    first = _head_lanes(0)

    def scores(unit):
        i, h = unit
        q_pair = q_ref[0, i * t:(i + 1) * t, :] * QK_SCALE
        return _dot_nt(_one_head(q_pair, h), k_ref[0, :(i + 1) * t, :])

    outs = {}

    def finish(unit, z):
        i, h = unit
        n = (i + 1) * t
        log_beta, log_rest = _log_sigmoid_pair(z)
        later, sums = [], []
        for c in range(i + 1):
            lr = log_rest[:, c * t:(c + 1) * t]
            if c == i:
                lr = jnp.where(strict, lr, 0.0)
            hi = lr.astype(_BF16)
            lo = (lr - hi.astype(_F32)).astype(_BF16)
            later.append(_dot(jnp.concatenate([hi, lo], axis=-1), later_ones2))
            sums.append(jnp.sum(lr, axis=-1, keepdims=True))
        tail = jnp.zeros((t, 1), _F32)
        chunks = [None] * (i + 1)
        for c in range(i, -1, -1):
            a = jnp.exp(log_beta[:, c * t:(c + 1) * t] + later[c] + tail)
            if c == i:
                a = jnp.where(strict, a, 0.0)
            chunks[c] = a.astype(_BF16)
            tail = tail + sums[c]
        a_all = chunks[0] if i == 0 else jnp.concatenate(chunks, axis=-1)
        outs[h] = _dot(a_all, v_ref[0, :n, :])
        if h == HEADS_PER_BLOCK - 1:
            o_ref[0, i * t:n, :] = jnp.where(first, outs[0], outs[1]).astype(o_ref.dtype)

    _pipelined([(i, h) for i in range(s_len // t) for h in range(HEADS_PER_BLOCK)], scores, finish)


def _diff_kernel(slopes_ref, q_ref, k_ref, v_ref, lam_ref, g_ref, o_ref, *, lam_init):
    t = ATT_T
    s_len = q_ref.shape[1]
    slope = slopes_ref[pl.program_id(1)]
    row, col = _triangle(t)
    causal = row >= col
    lq1, lk1, lq2, lk2 = (lam_ref[n:n + 1, :] for n in range(4))
    lam = (jnp.exp(jnp.sum(lq1 * lk1, axis=-1, keepdims=True))
           - jnp.exp(jnp.sum(lq2 * lk2, axis=-1, keepdims=True)) + lam_init)
    gain = g_ref[...] * (1.0 - lam_init)

    def scores(unit):
        i, h = unit
        n = (i + 1) * t
        q_pair = q_ref[0, i * t:n, :] * QK_SCALE
        key_bias = slope * (lax.broadcasted_iota(jnp.int32, (1, n), 1) - (n - 1)).astype(_F32)
        return _dot_nt(_one_head(q_pair, h), k_ref[0, :n, :]) + key_bias

    values = jnp.concatenate([v_ref[0], jnp.ones_like(v_ref[0])], axis=-1)
    maps = {}

    def finish(unit, s):
        i, h = unit
        n = (i + 1) * t
        p, _ = _causal_probs(s, causal, False)
        o = _dot(p.astype(_BF16), values[:n])
        maps[h] = o[:, :LANES] / o[:, LANES:]
        if h == 1:
            od = maps[0] - lam * maps[1]
            od = od * lax.rsqrt(jnp.mean(od * od, axis=-1, keepdims=True) + RMS_EPS)
            o_ref[0, i * t:n, :] = (od * gain).astype(o_ref.dtype)

    _pipelined([(i, h) for i in range(s_len // t) for h in range(2)], scores, finish)


def _dil_kernel(slopes_ref, q_ref, k_ref, v_ref, logm_ref, ndist_ref, o_ref, bias_ref):
    t = ATT_T
    s_len = q_ref.shape[1]
    first = _head_lanes(0)
    values = [_values_with_ones(v_ref[0], h) for h in range(HEADS_PER_BLOCK)]
    for h in range(HEADS_PER_BLOCK):
        slope = slopes_ref[N_DIFF_HEADS + HEADS_PER_BLOCK * pl.program_id(1) + h]
        bias_ref[h] = logm_ref[...] + slope * ndist_ref[...]

    def scores(unit):
        i, h = unit
        n = (i + 1) * t
        q_pair = q_ref[0, i * t:n, :] * QK_SCALE
        return _dot_nt(_one_head(q_pair, h), k_ref[0, :n, :]) + bias_ref[h, :, s_len - n:]

    outs = {}

    def finish(unit, s):
        i, h = unit
        n = (i + 1) * t
        p, _ = _causal_probs(s, None, False)
        outs[h] = _normalise(_dot(p.astype(_BF16), values[h][:n]))
        if h == HEADS_PER_BLOCK - 1:
            o_ref[0, i * t:n, :] = jnp.where(first, outs[0], outs[1]).astype(o_ref.dtype)

    _pipelined([(i, h) for i in range(s_len // t) for h in range(HEADS_PER_BLOCK)], scores, finish)


def _attention_call(kernel, h3, extra_specs, extra_args, segs, scalars=(), scratch_shapes=()):
    b, s, _ = h3.shape
    seg_q, seg_k, seg_v = segs

    def col_block(seg):
        return pl.BlockSpec((1, s, LANES), lambda bb, g, *_: (bb, 0, seg + g))

    grid_spec = pltpu.PrefetchScalarGridSpec(
        num_scalar_prefetch=len(scalars),
        grid=(b, GROUP_WIDTH // LANES),
        in_specs=[col_block(seg_q), col_block(seg_k), col_block(seg_v)] + list(extra_specs),
        out_specs=col_block(0),
        scratch_shapes=list(scratch_shapes))
    return pl.pallas_call(
        kernel,
        grid_spec=grid_spec,
        out_shape=jax.ShapeDtypeStruct((b, s, GROUP_WIDTH), _BF16),
        compiler_params=_params("arbitrary", "arbitrary"),
    )(*scalars, h3, h3, h3, *extra_args)


def _dilated_tables(s):
    r = lax.broadcasted_iota(jnp.int32, (ATT_T, s), 0)
    c = lax.broadcasted_iota(jnp.int32, (ATT_T, s), 1)
    dist = (s - ATT_T) + r - c
    mult = jnp.zeros(dist.shape, _F32)
    for window, dilation in DILATED_BRANCHES:
        hit = (dist >= 0) & (dist <= window) & (dist % dilation == 0)
        mult = mult + hit.astype(_F32)
    logm = jnp.where(mult > 0, jnp.log(jnp.maximum(mult, 1.0)), MASKED)
    return logm, -jnp.maximum(dist, 0).astype(_F32)


def _layer_norm(z, g, b):
    mu = jnp.mean(z, axis=-1, keepdims=True)
    zc = z - mu
    var = jnp.mean(zc * zc, axis=-1, keepdims=True)
    return zc * lax.rsqrt(var + LN_EPS) * g + b


def _out_proj_kernel(m0_ref, m1_ref, m2_ref, m3_ref, w_ref, x_ref, g_ref, b_ref, xo_ref, xob_ref):
    sub = OUT_PROJ_SUB_ROWS
    blocks = [slice(r, r + sub) for r in range(0, x_ref.shape[0], sub)]

    def project(rows):
        y = None
        for n, m_ref in enumerate((m0_ref, m1_ref, m2_ref, m3_ref)):
            part = _dot(m_ref[rows, :], w_ref[n * GROUP_WIDTH:(n + 1) * GROUP_WIDTH, :])
            y = part if y is None else y + part
        return y

    def finish(rows, y):
        out = _layer_norm(DEEPNORM_ALPHA * x_ref[rows, :] + y, g_ref[...], b_ref[...])
        xo_ref[rows, :] = out
        xob_ref[rows, :] = out.astype(_BF16)

    _pipelined(blocks, project, finish)


OUT_PROJ_SUB_ROWS = 128


def _out_projection(mixed, w, x, g, b, layer, tm=512):
    m, d = x.shape
    row = lambda i: (i, 0)
    const = lambda i: (0, 0)
    return pl.pallas_call(
        _out_proj_kernel,
        grid=(m // tm,),
        in_specs=[pl.BlockSpec((tm, GROUP_WIDTH), row)] * 4 + [
            pl.BlockSpec((None, d, d), lambda i: (layer, 0, 0)), pl.BlockSpec((tm, d), row),
            pl.BlockSpec((1, d), const), pl.BlockSpec((1, d), const)],
        out_specs=[pl.BlockSpec((tm, d), row), pl.BlockSpec((tm, d), row)],
        out_shape=[jax.ShapeDtypeStruct((m, d), _F32), jax.ShapeDtypeStruct((m, d), _BF16)],
        compiler_params=_params("arbitrary"),
    )(*mixed, w, x, g, b)


def _mlp_kernel(xb_ref, x_ref, w1_ref, w2_ref, g_ref, b_ref, xo_ref, xob_ref, acc_ref):
    f = pl.program_id(1)

    @pl.when(f == 0)
    def _():
        acc_ref[...] = jnp.zeros_like(acc_ref)

    a = jnp.maximum(_dot(xb_ref[...], w1_ref[...]), 0.0)
    acc_ref[...] += _dot((a * a).astype(_BF16), w2_ref[...])

    @pl.when(f == pl.num_programs(1) - 1)
    def _():
        out = _layer_norm(DEEPNORM_ALPHA * x_ref[...] + acc_ref[...], g_ref[...], b_ref[...])
        xo_ref[...] = out
        xob_ref[...] = out.astype(_BF16)


def _mlp(xb, x, w1, w2, g, b, layer, tm=512, tf=1024):
    m, d = x.shape
    ff = w1.shape[2]
    row = lambda i, f: (i, 0)
    const = lambda i, f: (0, 0)
    return pl.pallas_call(
        _mlp_kernel,
        grid=(m // tm, ff // tf),
        in_specs=[pl.BlockSpec((tm, d), row), pl.BlockSpec((tm, d), row),
                  pl.BlockSpec((None, d, tf), lambda i, f: (layer, 0, f)),
                  pl.BlockSpec((None, tf, d), lambda i, f: (layer, f, 0)),
                  pl.BlockSpec((1, d), const), pl.BlockSpec((1, d), const)],
        out_specs=[pl.BlockSpec((tm, d), row), pl.BlockSpec((tm, d), row)],
        out_shape=[jax.ShapeDtypeStruct((m, d), _F32), jax.ShapeDtypeStruct((m, d), _BF16)],
        scratch_shapes=[pltpu.VMEM((tm, d), _F32)],
        compiler_params=_params("arbitrary", "arbitrary"),
    )(xb, x, w1, w2, g, b)


def _alibi_slopes():
    n = N_DIFF_HEADS + N_DIL_HEADS
    return jnp.asarray(np.exp2(-ALIBI_MAX_EXP * np.arange(1, n + 1) / n), _F32)


def _layer(x, xb, layer_idx, wt_in, b_fz, lam_vecs, subln_g, w_out, ln1_g, ln1_b,
           w1, w2, ln2_g, ln2_b, slopes, dil_tables, batch):
    m, d = x.shape
    s = m // batch
    h3 = _in_projection(xb, wt_in, layer_idx).reshape(batch, s, -1)

    ct = _forget_cumsum(xb.reshape(batch, s, d), wt_in, b_fz, layer_idx)
    ct = ct.reshape(batch, N_FOX_HEADS, 1, s)
    o_fox = _attention_call(
        _fox_kernel, h3,
        [pl.BlockSpec((1, HEADS_PER_BLOCK, 1, s), lambda b, g: (b, g, 0, 0))],
        (ct,), (_SEG["fq"], _SEG["fk"], _SEG["fv"]))

    o_sb = _attention_call(_sb_kernel, h3, [], (), (_SEG["sq"], _SEG["sk"], _SEG["sv"]))

    lam_init = 0.8 - 0.6 * math.exp(-0.3 * layer_idx)
    o_diff = _attention_call(
        functools.partial(_diff_kernel, lam_init=lam_init), h3,
        [pl.BlockSpec((4, HEAD_DIM), lambda b, g, *_: (0, 0)),
         pl.BlockSpec((1, 2 * HEAD_DIM), lambda b, g, *_: (0, 0))],
        (lam_vecs, subln_g), (_SEG["dq"], _SEG["dk"], _SEG["dv"]), scalars=(slopes,))

    o_dil = _attention_call(
        _dil_kernel, h3,
        [pl.BlockSpec((ATT_T, s), lambda b, g, *_: (0, 0))] * 2,
        dil_tables, (_SEG["gq"], _SEG["gk"], _SEG["gv"]), scalars=(slopes,),
        scratch_shapes=[pltpu.VMEM((HEADS_PER_BLOCK, ATT_T, s), _F32)])

    mixed = [o.reshape(m, GROUP_WIDTH) for o in (o_fox, o_sb, o_diff, o_dil)]
    x, xb = _out_projection(mixed, w_out, x, ln1_g, ln1_b, layer_idx)
    return _mlp(xb, x, w1, w2, ln2_g, ln2_b, layer_idx)


def kernel(x, w_in, fox_forget_bias, diff_lambda_q1, diff_lambda_k1, diff_lambda_q2, diff_lambda_k2,
           diff_subln_gain, w_out, ln1_gain, ln1_bias, w_mlp_in, w_mlp_out, ln2_gain, ln2_bias):
    batch, s, d = x.shape
    assert d == D_MODEL and s % ATT_T == 0
    slopes = _alibi_slopes()
    dil_tables = _dilated_tables(s)
    xf = x.reshape(batch * s, d)
    xb = xf.astype(_BF16)
    wt_in = jnp.transpose(w_in, (0, 2, 1))
    w_out, w_mlp_in, w_mlp_out = (w.astype(_BF16) for w in (w_out, w_mlp_in, w_mlp_out))
    for l in range(DEPTH):
        b_fz = fox_forget_bias[l].reshape(N_FOX_HEADS, 1)
        lam_vecs = jnp.stack([diff_lambda_q1[l], diff_lambda_k1[l], diff_lambda_q2[l], diff_lambda_k2[l]])
        xf, xb = _layer(
            xf, xb, l, wt_in, b_fz, lam_vecs, diff_subln_gain[l].reshape(1, -1),
            w_out, ln1_gain[l].reshape(1, d), ln1_bias[l].reshape(1, d),
            w_mlp_in, w_mlp_out,
            ln2_gain[l].reshape(1, d), ln2_bias[l].reshape(1, d), slopes, dil_tables, batch)
    return xf.reshape(batch, s, d)
```

```python
import functools
import math

import numpy as np
import jax
import jax.numpy as jnp
from jax import lax
from jax.experimental import pallas as pl
from jax.experimental.pallas import tpu as pltpu

D_MODEL = 2048
DEPTH = 2
HEAD_DIM = 64
GROUP_WIDTH = 512
N_FOX_HEADS = 8
N_DIFF_HEADS = 4
N_DIL_HEADS = 8
D_FF = 4 * D_MODEL
DILATED_BRANCHES = ((128, 1), (512, 4), (2048, 16))
ALIBI_MAX_EXP = 8.0
LN_EPS = 1e-5
RMS_EPS = 1e-5
DEEPNORM_ALPHA = (2 * DEPTH) ** 0.25
QK_SCALE = HEAD_DIM ** -0.5

LANES = 128
HEADS_PER_BLOCK = LANES // HEAD_DIM
VMEM_LIMIT_BYTES = 56 * 1024 * 1024
MASKED = -1e30

_SEG = {name: 4 * i for i, name in enumerate(
    ("fq", "fk", "fv", "sq", "sk", "sv", "dq", "dk", "dv", "gq", "gk", "gv"))}

ATT_T = 256

_F32 = jnp.float32
_BF16 = jnp.bfloat16


def _params(*sem):
    return pltpu.CompilerParams(dimension_semantics=sem, vmem_limit_bytes=VMEM_LIMIT_BYTES)


def _dot(a, b):
    return jnp.dot(a, b, preferred_element_type=_F32)


def _dot_nt(a, b):
    return lax.dot_general(a, b, (((1,), (1,)), ((), ())), preferred_element_type=_F32)


FORGET_ROW0 = 3 * GROUP_WIDTH


def _in_proj_kernel(x_ref, wt_ref, o_ref, wb_ref):
    @pl.when(pl.program_id(1) == 0)
    def _():
        wb_ref[...] = wt_ref[0].astype(_BF16)

    o_ref[...] = _dot_nt(x_ref[...], wb_ref[...]).astype(o_ref.dtype)


def _in_projection(xb, wt, layer, tm=2048, tn=768):
    m, k = xb.shape
    n = wt.shape[1] - N_FOX_HEADS
    assert FORGET_ROW0 % tn == 0

    def w_rows(j, i):
        groups = j * (tn // N_FOX_HEADS) + jnp.where(j * tn >= FORGET_ROW0, 1, 0)
        return (layer, groups * N_FOX_HEADS, 0)

    return pl.pallas_call(
        _in_proj_kernel,
        grid=(n // tn, m // tm),
        in_specs=[pl.BlockSpec((tm, k), lambda j, i: (i, 0)),
                  pl.BlockSpec((pl.Element(1), pl.Element(tn), pl.Element(k)), w_rows)],
        out_specs=pl.BlockSpec((tm, tn), lambda j, i: (i, j)),
        out_shape=jax.ShapeDtypeStruct((m, n), _BF16),
        scratch_shapes=[pltpu.VMEM((tn, k), _BF16)],
        compiler_params=_params("arbitrary", "arbitrary"),
    )(xb, wt)


def _split3(x):
    hi = x.astype(_BF16)
    r = x - hi.astype(_F32)
    mid = r.astype(_BF16)
    lo = (r - mid.astype(_F32)).astype(_BF16)
    return hi, mid, lo


def _log_sigmoid_pair(z):
    log_sig = jnp.minimum(z, 0.0) - jnp.log(1.0 + jnp.exp(-jnp.abs(z)))
    return log_sig, log_sig - z


def _forget_kernel(x_ref, w_ref, b_ref, ct_ref, *, blk):
    s = x_ref.shape[1]
    z = _dot_nt(w_ref[...].astype(_BF16), x_ref[0]) + b_ref[...]
    log_f, _ = _log_sigmoid_pair(z)
    r = lax.broadcasted_iota(jnp.int32, (blk, blk), 0)
    c = lax.broadcasted_iota(jnp.int32, (blk, blk), 1)
    upto = (r <= c).astype(_BF16)
    carry = jnp.zeros((N_FOX_HEADS, 1), _F32)
    for i in range(s // blk):
        hi, mid, lo = _split3(log_f[:, i * blk:(i + 1) * blk])
        cum = _dot(hi, upto) + _dot(mid, upto) + _dot(lo, upto) + carry
        ct_ref[0, :, i * blk:(i + 1) * blk] = cum
        carry = cum[:, blk - 1:blk]


def _forget_cumsum(xb3, wt, b_fz, layer, blk=256):
    b, s, d = xb3.shape
    return pl.pallas_call(
        functools.partial(_forget_kernel, blk=blk),
        grid=(b,),
        in_specs=[pl.BlockSpec((1, s, d), lambda i: (i, 0, 0)),
                  pl.BlockSpec((None, N_FOX_HEADS, d), lambda i: (layer, FORGET_ROW0 // N_FOX_HEADS, 0)),
                  pl.BlockSpec((N_FOX_HEADS, 1), lambda i: (0, 0))],
        out_specs=pl.BlockSpec((1, N_FOX_HEADS, s), lambda i: (i, 0, 0)),
        out_shape=jax.ShapeDtypeStruct((b, N_FOX_HEADS, s), _F32),
        compiler_params=_params("arbitrary"),
    )(xb3, wt, b_fz)


def _triangle(t):
    r = lax.broadcasted_iota(jnp.int32, (t, t), 0)
    c = lax.broadcasted_iota(jnp.int32, (t, t), 1)
    return r, c


def _head_lanes(h):
    lane = lax.broadcasted_iota(jnp.int32, (1, LANES), 1)
    return (lane >= h * HEAD_DIM) & (lane < (h + 1) * HEAD_DIM)


def _one_head(q_pair, h):
    return jnp.where(_head_lanes(h), q_pair, jnp.zeros_like(q_pair))


def _causal_probs(s, causal, with_sums):
    t, n = s.shape
    diag = s[:, n - t:] if causal is None else jnp.where(causal, s[:, n - t:], MASKED)
    m = jnp.max(diag, axis=-1, keepdims=True)
    if n > t:
        left = s[:, :n - t]
        m = jnp.maximum(m, jnp.max(left, axis=-1, keepdims=True))
    p_diag = jnp.exp(diag - m)
    l = jnp.sum(p_diag, axis=-1, keepdims=True) if with_sums else None
    if n == t:
        return p_diag, l
    p_left = jnp.exp(left - m)
    if with_sums:
        l = l + jnp.sum(p_left, axis=-1, keepdims=True)
    return jnp.concatenate([p_left, p_diag], axis=-1), l


def _values_with_ones(v_pair, h):
    return jnp.where(_head_lanes(h), v_pair, jnp.ones_like(v_pair))


def _normalise(o):
    return o / pltpu.roll(o, HEAD_DIM, 1)


def _pipelined(units, scores, finish):
    pending = scores(units[0])
    for u, unit in enumerate(units):
        ahead = scores(units[u + 1]) if u + 1 < len(units) else None
        finish(unit, pending)
        pending = ahead


def _fox_kernel(q_ref, k_ref, v_ref, ct_ref, o_ref):
    t = ATT_T
    s_len = q_ref.shape[1]
    row, col = _triangle(t)
    causal = row >= col
    first = _head_lanes(0)
    values = [_values_with_ones(v_ref[0], h) for h in range(HEADS_PER_BLOCK)]

    def scores(unit):
        i, h = unit
        n = (i + 1) * t
        c_row = ct_ref[0, h, :, :n]
        key_bias = c_row[:, n - 1:n] - c_row
        q_pair = q_ref[0, i * t:n, :] * QK_SCALE
        return _dot_nt(_one_head(q_pair, h), k_ref[0, :n, :]) + key_bias

    outs = {}

    def finish(unit, s):
        i, h = unit
        n = (i + 1) * t
        p, _ = _causal_probs(s, causal, False)
        outs[h] = _normalise(_dot(p.astype(_BF16), values[h][:n]))
        if h == HEADS_PER_BLOCK - 1:
            o_ref[0, i * t:n, :] = jnp.where(first, outs[0], outs[1]).astype(o_ref.dtype)

    _pipelined([(i, h) for i in range(s_len // t) for h in range(HEADS_PER_BLOCK)], scores, finish)


def _sb_kernel(q_ref, k_ref, v_ref, o_ref):
    t = ATT_T
    s_len = q_ref.shape[1]
    row, col = _triangle(t)
    strict = row > col
    later_ones = strict.astype(_BF16)
    later_ones2 = jnp.concatenate([later_ones, later_ones], axis=0)
    first = _head_lanes(0)

    def scores(unit):
        i, h = unit
        q_pair = q_ref[0, i * t:(i + 1) * t, :] * QK_SCALE
        return _dot_nt(_one_head(q_pair, h), k_ref[0, :(i + 1) * t, :])

    outs = {}

    def finish(unit, z):
        i, h = unit
        n = (i + 1) * t
        log_beta, log_rest = _log_sigmoid_pair(z)
        later, sums = [], []
        for c in range(i + 1):
            lr = log_rest[:, c * t:(c + 1) * t]
            if c == i:
                lr = jnp.where(strict, lr, 0.0)
            hi = lr.astype(_BF16)
            lo = (lr - hi.astype(_F32)).astype(_BF16)
            later.append(_dot(jnp.concatenate([hi, lo], axis=-1), later_ones2))
            sums.append(jnp.sum(lr, axis=-1, keepdims=True))
        tail = jnp.zeros((t, 1), _F32)
        chunks = [None] * (i + 1)
        for c in range(i, -1, -1):
            a = jnp.exp(log_beta[:, c * t:(c + 1) * t] + later[c] + tail)
            if c == i:
                a = jnp.where(strict, a, 0.0)
            chunks[c] = a.astype(_BF16)
            tail = tail + sums[c]
        a_all = chunks[0] if i == 0 else jnp.concatenate(chunks, axis=-1)
        outs[h] = _dot(a_all, v_ref[0, :n, :])
        if h == HEADS_PER_BLOCK - 1:
            o_ref[0, i * t:n, :] = jnp.where(first, outs[0], outs[1]).astype(o_ref.dtype)

    _pipelined([(i, h) for i in range(s_len // t) for h in range(HEADS_PER_BLOCK)], scores, finish)


def _diff_kernel(slopes_ref, q_ref, k_ref, v_ref, lam_ref, g_ref, o_ref, *, lam_init):
    t = ATT_T
    s_len = q_ref.shape[1]
    slope = slopes_ref[pl.program_id(1)]
    row, col = _triangle(t)
    causal = row >= col
    lq1, lk1, lq2, lk2 = (lam_ref[n:n + 1, :] for n in range(4))
    lam = (jnp.exp(jnp.sum(lq1 * lk1, axis=-1, keepdims=True))
           - jnp.exp(jnp.sum(lq2 * lk2, axis=-1, keepdims=True)) + lam_init)
    gain = g_ref[...] * (1.0 - lam_init)

    def scores(unit):
        i, h = unit
        n = (i + 1) * t
        q_pair = q_ref[0, i * t:n, :] * QK_SCALE
        key_bias = slope * (lax.broadcasted_iota(jnp.int32, (1, n), 1) - (n - 1)).astype(_F32)
        return _dot_nt(_one_head(q_pair, h), k_ref[0, :n, :]) + key_bias

    values = jnp.concatenate([v_ref[0], jnp.ones_like(v_ref[0])], axis=-1)
    maps = {}

    def finish(unit, s):
        i, h = unit
        n = (i + 1) * t
        p, _ = _causal_probs(s, causal, False)
        o = _dot(p.astype(_BF16), values[:n])
        maps[h] = o[:, :LANES] / o[:, LANES:]
        if h == 1:
            od = maps[0] - lam * maps[1]
            od = od * lax.rsqrt(jnp.mean(od * od, axis=-1, keepdims=True) + RMS_EPS)
            o_ref[0, i * t:n, :] = (od * gain).astype(o_ref.dtype)

    _pipelined([(i, h) for i in range(s_len // t) for h in range(2)], scores, finish)


def _dil_kernel(slopes_ref, q_ref, k_ref, v_ref, logm_ref, ndist_ref, o_ref, bias_ref):
    t = ATT_T
    s_len = q_ref.shape[1]
    first = _head_lanes(0)
    values = [_values_with_ones(v_ref[0], h) for h in range(HEADS_PER_BLOCK)]
    for h in range(HEADS_PER_BLOCK):
        slope = slopes_ref[N_DIFF_HEADS + HEADS_PER_BLOCK * pl.program_id(1) + h]
        bias_ref[h] = logm_ref[...] + slope * ndist_ref[...]

    def scores(unit):
        i, h = unit
        n = (i + 1) * t
        q_pair = q_ref[0, i * t:n, :] * QK_SCALE
        return _dot_nt(_one_head(q_pair, h), k_ref[0, :n, :]) + bias_ref[h, :, s_len - n:]

    outs = {}

    def finish(unit, s):
        i, h = unit
        n = (i + 1) * t
        p, _ = _causal_probs(s, None, False)
        outs[h] = _normalise(_dot(p.astype(_BF16), values[h][:n]))
        if h == HEADS_PER_BLOCK - 1:
            o_ref[0, i * t:n, :] = jnp.where(first, outs[0], outs[1]).astype(o_ref.dtype)

    _pipelined([(i, h) for i in range(s_len // t) for h in range(HEADS_PER_BLOCK)], scores, finish)


def _attention_call(kernel, h3, extra_specs, extra_args, segs, scalars=(), scratch_shapes=()):
    b, s, _ = h3.shape
    seg_q, seg_k, seg_v = segs

    def col_block(seg):
        return pl.BlockSpec((1, s, LANES), lambda bb, g, *_: (bb, 0, seg + g))

    grid_spec = pltpu.PrefetchScalarGridSpec(
        num_scalar_prefetch=len(scalars),
        grid=(b, GROUP_WIDTH // LANES),
        in_specs=[col_block(seg_q), col_block(seg_k), col_block(seg_v)] + list(extra_specs),
        out_specs=col_block(0),
        scratch_shapes=list(scratch_shapes))
    return pl.pallas_call(
        kernel,
        grid_spec=grid_spec,
        out_shape=jax.ShapeDtypeStruct((b, s, GROUP_WIDTH), _BF16),
        compiler_params=_params("arbitrary", "arbitrary"),
    )(*scalars, h3, h3, h3, *extra_args)


def _dilated_tables(s):
    r = lax.broadcasted_iota(jnp.int32, (ATT_T, s), 0)
    c = lax.broadcasted_iota(jnp.int32, (ATT_T, s), 1)
    dist = (s - ATT_T) + r - c
    mult = jnp.zeros(dist.shape, _F32)
    for window, dilation in DILATED_BRANCHES:
        hit = (dist >= 0) & (dist <= window) & (dist % dilation == 0)
        mult = mult + hit.astype(_F32)
    logm = jnp.where(mult > 0, jnp.log(jnp.maximum(mult, 1.0)), MASKED)
    return logm, -jnp.maximum(dist, 0).astype(_F32)


def _layer_norm(z, g, b):
    mu = jnp.mean(z, axis=-1, keepdims=True)
    zc = z - mu
    var = jnp.mean(zc * zc, axis=-1, keepdims=True)
    return zc * lax.rsqrt(var + LN_EPS) * g + b


def _out_proj_kernel(m0_ref, m1_ref, m2_ref, m3_ref, w_ref, x_ref, g_ref, b_ref, xo_ref, xob_ref):
    sub = OUT_PROJ_SUB_ROWS
    blocks = [slice(r, r + sub) for r in range(0, x_ref.shape[0], sub)]

    def project(rows):
        y = None
        for n, m_ref in enumerate((m0_ref, m1_ref, m2_ref, m3_ref)):
            part = _dot(m_ref[rows, :], w_ref[n * GROUP_WIDTH:(n + 1) * GROUP_WIDTH, :])
            y = part if y is None else y + part
        return y

    def finish(rows, y):
        out = _layer_norm(DEEPNORM_ALPHA * x_ref[rows, :] + y, g_ref[...], b_ref[...])
        xo_ref[rows, :] = out
        xob_ref[rows, :] = out.astype(_BF16)

    _pipelined(blocks, project, finish)


OUT_PROJ_SUB_ROWS = 128


def _out_projection(mixed, w, x, g, b, layer, tm=512):
    m, d = x.shape
    row = lambda i: (i, 0)
    const = lambda i: (0, 0)
    return pl.pallas_call(
        _out_proj_kernel,
        grid=(m // tm,),
        in_specs=[pl.BlockSpec((tm, GROUP_WIDTH), row)] * 4 + [
            pl.BlockSpec((None, d, d), lambda i: (layer, 0, 0)), pl.BlockSpec((tm, d), row),
            pl.BlockSpec((1, d), const), pl.BlockSpec((1, d), const)],
        out_specs=[pl.BlockSpec((tm, d), row), pl.BlockSpec((tm, d), row)],
        out_shape=[jax.ShapeDtypeStruct((m, d), _F32), jax.ShapeDtypeStruct((m, d), _BF16)],
        compiler_params=_params("arbitrary"),
    )(*mixed, w, x, g, b)


def _mlp_kernel(xb_ref, x_ref, w1_hbm, w2_hbm, g_ref, b_ref, xo_ref, xob_ref,
                w1_buf, w2_buf, acc_ref, sems, *, layer, tf):
    n_chunks = w1_hbm.shape[2] // tf
    assert n_chunks % 2 == 0 and n_chunks >= 2
    step, n_steps = pl.program_id(0), pl.num_programs(0)

    def w1_copy(c):
        return pltpu.make_async_copy(w1_hbm.at[layer, :, pl.ds(c * tf, tf)], w1_buf.at[c % 2], sems.at[0, c % 2])

    def w2_copy(c):
        return pltpu.make_async_copy(w2_hbm.at[layer, pl.ds(c * tf, tf), :], w2_buf.at[c % 2], sems.at[1, c % 2])

    def start(copy_of, c):
        if c < n_chunks:
            copy_of(c).start()
        else:
            pl.when(step + 1 < n_steps)(lambda: copy_of(c - n_chunks).start())

    @pl.when(step == 0)
    def _():
        w1_copy(0).start()
        w2_copy(0).start()
        w1_copy(1).start()

    w1_copy(0).wait()
    hidden = _dot(xb_ref[...], w1_buf[0])
    for c in range(n_chunks):
        start(w1_copy, c + 2)
        start(w2_copy, c + 1)
        if c + 1 < n_chunks:
            w1_copy(c + 1).wait()
        w2_copy(c).wait()
        ahead = _dot(xb_ref[...], w1_buf[(c + 1) % 2]) if c + 1 < n_chunks else None
        a = jnp.maximum(hidden, 0.0)
        part = _dot((a * a).astype(_BF16), w2_buf[c % 2])
        if c == 0:
            acc_ref[...] = part
        else:
            acc_ref[...] += part
        hidden = ahead

    out = _layer_norm(DEEPNORM_ALPHA * x_ref[...] + acc_ref[...], g_ref[...], b_ref[...])
    xo_ref[...] = out
    xob_ref[...] = out.astype(_BF16)


def _mlp(xb, x, w1, w2, g, b, layer, tm=512, tf=1024):
    m, d = x.shape
    row = lambda i: (i, 0)
    const = lambda i: (0, 0)
    hbm = pl.BlockSpec(memory_space=pl.ANY)
    return pl.pallas_call(
        functools.partial(_mlp_kernel, layer=layer, tf=tf),
        grid=(m // tm,),
        in_specs=[pl.BlockSpec((tm, d), row), pl.BlockSpec((tm, d), row), hbm, hbm,
                  pl.BlockSpec((1, d), const), pl.BlockSpec((1, d), const)],
        out_specs=[pl.BlockSpec((tm, d), row), pl.BlockSpec((tm, d), row)],
        out_shape=[jax.ShapeDtypeStruct((m, d), _F32), jax.ShapeDtypeStruct((m, d), _BF16)],
        scratch_shapes=[pltpu.VMEM((2, d, tf), _BF16), pltpu.VMEM((2, tf, d), _BF16),
                        pltpu.VMEM((tm, d), _F32), pltpu.SemaphoreType.DMA((2, 2))],
        compiler_params=_params("arbitrary"),
    )(xb, x, w1, w2, g, b)


def _alibi_slopes():
    n = N_DIFF_HEADS + N_DIL_HEADS
    return jnp.asarray(np.exp2(-ALIBI_MAX_EXP * np.arange(1, n + 1) / n), _F32)


def _layer(x, xb, layer_idx, wt_in, b_fz, lam_vecs, subln_g, w_out, ln1_g, ln1_b,
           w1, w2, ln2_g, ln2_b, slopes, dil_tables, batch):
    m, d = x.shape
    s = m // batch
    h3 = _in_projection(xb, wt_in, layer_idx).reshape(batch, s, -1)

    ct = _forget_cumsum(xb.reshape(batch, s, d), wt_in, b_fz, layer_idx)
    ct = ct.reshape(batch, N_FOX_HEADS, 1, s)
    o_fox = _attention_call(
        _fox_kernel, h3,
        [pl.BlockSpec((1, HEADS_PER_BLOCK, 1, s), lambda b, g: (b, g, 0, 0))],
        (ct,), (_SEG["fq"], _SEG["fk"], _SEG["fv"]))

    o_sb = _attention_call(_sb_kernel, h3, [], (), (_SEG["sq"], _SEG["sk"], _SEG["sv"]))

    lam_init = 0.8 - 0.6 * math.exp(-0.3 * layer_idx)
    o_diff = _attention_call(
        functools.partial(_diff_kernel, lam_init=lam_init), h3,
        [pl.BlockSpec((4, HEAD_DIM), lambda b, g, *_: (0, 0)),
         pl.BlockSpec((1, 2 * HEAD_DIM), lambda b, g, *_: (0, 0))],
        (lam_vecs, subln_g), (_SEG["dq"], _SEG["dk"], _SEG["dv"]), scalars=(slopes,))

    o_dil = _attention_call(
        _dil_kernel, h3,
        [pl.BlockSpec((ATT_T, s), lambda b, g, *_: (0, 0))] * 2,
        dil_tables, (_SEG["gq"], _SEG["gk"], _SEG["gv"]), scalars=(slopes,),
        scratch_shapes=[pltpu.VMEM((HEADS_PER_BLOCK, ATT_T, s), _F32)])

    mixed = [o.reshape(m, GROUP_WIDTH) for o in (o_fox, o_sb, o_diff, o_dil)]
    x, xb = _out_projection(mixed, w_out, x, ln1_g, ln1_b, layer_idx)
    return _mlp(xb, x, w1, w2, ln2_g, ln2_b, layer_idx)


def kernel(x, w_in, fox_forget_bias, diff_lambda_q1, diff_lambda_k1, diff_lambda_q2, diff_lambda_k2,
           diff_subln_gain, w_out, ln1_gain, ln1_bias, w_mlp_in, w_mlp_out, ln2_gain, ln2_bias):
    batch, s, d = x.shape
    assert d == D_MODEL and s % ATT_T == 0
    slopes = _alibi_slopes()
    dil_tables = _dilated_tables(s)
    xf = x.reshape(batch * s, d)
    xb = xf.astype(_BF16)
    wt_in = jnp.transpose(w_in, (0, 2, 1))
    w_out, w_mlp_in, w_mlp_out = (w.astype(_BF16) for w in (w_out, w_mlp_in, w_mlp_out))
    for l in range(DEPTH):
        b_fz = fox_forget_bias[l].reshape(N_FOX_HEADS, 1)
        lam_vecs = jnp.stack([diff_lambda_q1[l], diff_lambda_k1[l], diff_lambda_q2[l], diff_lambda_k2[l]])
        xf, xb = _layer(
            xf, xb, l, wt_in, b_fz, lam_vecs, diff_subln_gain[l].reshape(1, -1),
            w_out, ln1_gain[l].reshape(1, d), ln1_bias[l].reshape(1, d),
            w_mlp_in, w_mlp_out,
            ln2_gain[l].reshape(1, d), ln2_bias[l].reshape(1, d), slopes, dil_tables, batch)
    return xf.reshape(batch, s, d)
```

```python
import functools
import math

import numpy as np
import jax
import jax.numpy as jnp
from jax import lax
from jax.experimental import pallas as pl
from jax.experimental.pallas import tpu as pltpu

D_MODEL = 2048
DEPTH = 2
HEAD_DIM = 64
GROUP_WIDTH = 512
N_FOX_HEADS = 8
N_DIFF_HEADS = 4
N_DIL_HEADS = 8
D_FF = 4 * D_MODEL
DILATED_BRANCHES = ((128, 1), (512, 4), (2048, 16))
ALIBI_MAX_EXP = 8.0
LN_EPS = 1e-5
RMS_EPS = 1e-5
DEEPNORM_ALPHA = (2 * DEPTH) ** 0.25
QK_SCALE = HEAD_DIM ** -0.5

LANES = 128
HEADS_PER_BLOCK = LANES // HEAD_DIM
VMEM_LIMIT_BYTES = 56 * 1024 * 1024
MASKED = -1e30

_SEG = {name: 4 * i for i, name in enumerate(
    ("fq", "fk", "fv", "sq", "sk", "sv", "dq", "dk", "dv", "gq", "gk", "gv"))}

ATT_T = 256
ATT_LOOKAHEAD = 1

_F32 = jnp.float32
_BF16 = jnp.bfloat16


def _params(*sem):
    return pltpu.CompilerParams(dimension_semantics=sem, vmem_limit_bytes=VMEM_LIMIT_BYTES)


def _dot(a, b):
    return jnp.dot(a, b, preferred_element_type=_F32)


def _dot_nt(a, b):
    return lax.dot_general(a, b, (((1,), (1,)), ((), ())), preferred_element_type=_F32)


FORGET_ROW0 = 3 * GROUP_WIDTH


def _in_proj_kernel(x_ref, wt_ref, o_ref, wb_ref):
    @pl.when(pl.program_id(1) == 0)
    def _():
        wb_ref[...] = wt_ref[0].astype(_BF16)

    o_ref[...] = _dot_nt(x_ref[...], wb_ref[...]).astype(o_ref.dtype)


def _in_projection(xb, wt, layer, tm=2048, tn=768):
    m, k = xb.shape
    n = wt.shape[1] - N_FOX_HEADS
    assert FORGET_ROW0 % tn == 0

    def w_rows(j, i):
        groups = j * (tn // N_FOX_HEADS) + jnp.where(j * tn >= FORGET_ROW0, 1, 0)
        return (layer, groups * N_FOX_HEADS, 0)

    return pl.pallas_call(
        _in_proj_kernel,
        grid=(n // tn, m // tm),
        in_specs=[pl.BlockSpec((tm, k), lambda j, i: (i, 0)),
                  pl.BlockSpec((pl.Element(1), pl.Element(tn), pl.Element(k)), w_rows)],
        out_specs=pl.BlockSpec((tm, tn), lambda j, i: (i, j)),
        out_shape=jax.ShapeDtypeStruct((m, n), _BF16),
        scratch_shapes=[pltpu.VMEM((tn, k), _BF16)],
        compiler_params=_params("arbitrary", "arbitrary"),
    )(xb, wt)


def _split3(x):
    hi = x.astype(_BF16)
    r = x - hi.astype(_F32)
    mid = r.astype(_BF16)
    lo = (r - mid.astype(_F32)).astype(_BF16)
    return hi, mid, lo


def _log_sigmoid_pair(z):
    log_sig = jnp.minimum(z, 0.0) - jnp.log(1.0 + jnp.exp(-jnp.abs(z)))
    return log_sig, log_sig - z


def _forget_kernel(x_ref, w_ref, b_ref, wo_ref, ct_ref, wob_ref, *, blk):
    wob_ref[...] = wo_ref[...].astype(wob_ref.dtype)
    s = x_ref.shape[1]
    z = _dot_nt(w_ref[...].astype(_BF16), x_ref[0]) + b_ref[...]
    log_f, _ = _log_sigmoid_pair(z)
    r = lax.broadcasted_iota(jnp.int32, (blk, blk), 0)
    c = lax.broadcasted_iota(jnp.int32, (blk, blk), 1)
    upto = (r <= c).astype(_BF16)
    carry = jnp.zeros((N_FOX_HEADS, 1), _F32)
    for i in range(s // blk):
        hi, mid, lo = _split3(log_f[:, i * blk:(i + 1) * blk])
        cum = _dot(hi, upto) + _dot(mid, upto) + _dot(lo, upto) + carry
        ct_ref[0, :, i * blk:(i + 1) * blk] = cum
        carry = cum[:, blk - 1:blk]


def _forget_cumsum(xb3, wt, b_fz, w_out, layer, blk=256):
    b, s, d = xb3.shape
    slab = w_out.shape[1] // b
    return pl.pallas_call(
        functools.partial(_forget_kernel, blk=blk),
        grid=(b,),
        in_specs=[pl.BlockSpec((1, s, d), lambda i: (i, 0, 0)),
                  pl.BlockSpec((None, N_FOX_HEADS, d), lambda i: (layer, FORGET_ROW0 // N_FOX_HEADS, 0)),
                  pl.BlockSpec((N_FOX_HEADS, 1), lambda i: (0, 0)),
                  pl.BlockSpec((None, slab, w_out.shape[2]), lambda i: (layer, i, 0))],
        out_specs=[pl.BlockSpec((1, N_FOX_HEADS, s), lambda i: (i, 0, 0)),
                   pl.BlockSpec((slab, w_out.shape[2]), lambda i: (i, 0))],
        out_shape=[jax.ShapeDtypeStruct((b, N_FOX_HEADS, s), _F32),
                   jax.ShapeDtypeStruct(w_out.shape[1:], _BF16)],
        compiler_params=_params("arbitrary"),
    )(xb3, wt, b_fz, w_out)


def _triangle(t):
    r = lax.broadcasted_iota(jnp.int32, (t, t), 0)
    c = lax.broadcasted_iota(jnp.int32, (t, t), 1)
    return r, c


def _head_lanes(h):
    lane = lax.broadcasted_iota(jnp.int32, (1, LANES), 1)
    return (lane >= h * HEAD_DIM) & (lane < (h + 1) * HEAD_DIM)


def _one_head(q_pair, h):
    return jnp.where(_head_lanes(h), q_pair, jnp.zeros_like(q_pair))


def _causal_probs(s, causal, with_sums):
    t, n = s.shape
    diag = s[:, n - t:] if causal is None else jnp.where(causal, s[:, n - t:], MASKED)
    m = jnp.max(diag, axis=-1, keepdims=True)
    if n > t:
        left = s[:, :n - t]
        m = jnp.maximum(m, jnp.max(left, axis=-1, keepdims=True))
    p_diag = jnp.exp(diag - m)
    l = jnp.sum(p_diag, axis=-1, keepdims=True) if with_sums else None
    if n == t:
        return p_diag, l
    p_left = jnp.exp(left - m)
    if with_sums:
        l = l + jnp.sum(p_left, axis=-1, keepdims=True)
    return jnp.concatenate([p_left, p_diag], axis=-1), l


def _values_with_ones(v_pair, h):
    return jnp.where(_head_lanes(h), v_pair, jnp.ones_like(v_pair))


def _normalise(o):
    return o / pltpu.roll(o, HEAD_DIM, 1)


def _pipelined(units, scores, finish, lookahead=1):
    pending = [scores(unit) for unit in units[:lookahead]]
    for u, unit in enumerate(units):
        if u + lookahead < len(units):
            pending.append(scores(units[u + lookahead]))
        finish(unit, pending.pop(0))


def _fox_kernel(q_ref, k_ref, v_ref, ct_ref, o_ref):
    t = ATT_T
    s_len = q_ref.shape[1]
    row, col = _triangle(t)
    causal = row >= col
    first = _head_lanes(0)
    values = [_values_with_ones(v_ref[0], h) for h in range(HEADS_PER_BLOCK)]

    def scores(unit):
        i, h = unit
        n = (i + 1) * t
        c_row = ct_ref[0, h, :, :n]
        key_bias = c_row[:, n - 1:n] - c_row
        q_pair = q_ref[0, i * t:n, :] * QK_SCALE
        return _dot_nt(_one_head(q_pair, h), k_ref[0, :n, :]) + key_bias

    outs = {}

    def finish(unit, s):
        i, h = unit
        n = (i + 1) * t
        p, _ = _causal_probs(s, causal, False)
        outs[h] = _normalise(_dot(p.astype(_BF16), values[h][:n]))
        if h == HEADS_PER_BLOCK - 1:
            o_ref[0, i * t:n, :] = jnp.where(first, outs[0], outs[1]).astype(o_ref.dtype)

    _pipelined([(i, h) for i in range(s_len // t) for h in range(HEADS_PER_BLOCK)], scores, finish,
               ATT_LOOKAHEAD)


def _sb_kernel(q_ref, k_ref, v_ref, o_ref):
    t = ATT_T
    s_len = q_ref.shape[1]
    row, col = _triangle(t)
    strict = row > col
    from_ones = (row >= col).astype(_BF16)
    from_ones2 = jnp.concatenate([from_ones, from_ones], axis=0)
    first = _head_lanes(0)

    def scores(unit):
        i, h = unit
        q_pair = q_ref[0, i * t:(i + 1) * t, :] * QK_SCALE
        return _dot_nt(_one_head(q_pair, h), k_ref[0, :(i + 1) * t, :])

    outs = {}

    def finish(unit, z):
        i, h = unit
        n = (i + 1) * t
        nz = -z
        log_rest = jnp.minimum(nz, 0.0) - jnp.log(1.0 + jnp.exp(jnp.minimum(z, nz)))
        incl, sums = [], []
        for c in range(i + 1):
            lr = log_rest[:, c * t:(c + 1) * t]
            if c == i:
                lr = jnp.where(strict, lr, 0.0)
            hi = lr.astype(_BF16)
            lo = (lr - hi.astype(_F32)).astype(_BF16)
            incl.append(_dot(jnp.concatenate([hi, lo], axis=-1), from_ones2))
            sums.append(incl[c][:, :1])
        tail = jnp.zeros((t, 1), _F32)
        chunks = [None] * (i + 1)
        for c in range(i, -1, -1):
            a = jnp.exp(z[:, c * t:(c + 1) * t] + incl[c] + tail)
            if c == i:
                a = jnp.where(strict, a, 0.0)
            chunks[c] = a.astype(_BF16)
            tail = tail + sums[c]
        a_all = chunks[0] if i == 0 else jnp.concatenate(chunks, axis=-1)
        outs[h] = _dot(a_all, v_ref[0, :n, :])
        if h == HEADS_PER_BLOCK - 1:
            o_ref[0, i * t:n, :] = jnp.where(first, outs[0], outs[1]).astype(o_ref.dtype)

    _pipelined([(i, h) for i in range(s_len // t) for h in range(HEADS_PER_BLOCK)], scores, finish,
               ATT_LOOKAHEAD)


def _diff_kernel(slopes_ref, q_ref, k_ref, v_ref, lam_ref, g_ref, o_ref, *, lam_init):
    t = ATT_T
    s_len = q_ref.shape[1]
    slope = slopes_ref[pl.program_id(1)]
    row, col = _triangle(t)
    causal = row >= col
    lq1, lk1, lq2, lk2 = (lam_ref[n:n + 1, :] for n in range(4))
    lam = (jnp.exp(jnp.sum(lq1 * lk1, axis=-1, keepdims=True))
           - jnp.exp(jnp.sum(lq2 * lk2, axis=-1, keepdims=True)) + lam_init)
    gain = g_ref[...] * (1.0 - lam_init)

    def scores(unit):
        i, h = unit
        n = (i + 1) * t
        q_pair = q_ref[0, i * t:n, :] * QK_SCALE
        key_bias = slope * (lax.broadcasted_iota(jnp.int32, (1, n), 1) - (n - 1)).astype(_F32)
        return _dot_nt(_one_head(q_pair, h), k_ref[0, :n, :]) + key_bias

    values = jnp.concatenate([v_ref[0], jnp.ones_like(v_ref[0])], axis=-1)
    maps = {}

    def finish(unit, s):
        i, h = unit
        n = (i + 1) * t
        p, _ = _causal_probs(s, causal, False)
        o = _dot(p.astype(_BF16), values[:n])
        maps[h] = o[:, :LANES] / o[:, LANES:]
        if h == 1:
            od = maps[0] - lam * maps[1]
            od = od * lax.rsqrt(jnp.mean(od * od, axis=-1, keepdims=True) + RMS_EPS)
            o_ref[0, i * t:n, :] = (od * gain).astype(o_ref.dtype)

    _pipelined([(i, h) for i in range(s_len // t) for h in range(2)], scores, finish, ATT_LOOKAHEAD)


def _dil_kernel(slopes_ref, q_ref, k_ref, v_ref, logm_ref, ndist_ref, o_ref, bias_ref):
    t = ATT_T
    s_len = q_ref.shape[1]
    first = _head_lanes(0)
    values = [_values_with_ones(v_ref[0], h) for h in range(HEADS_PER_BLOCK)]
    for h in range(HEADS_PER_BLOCK):
        slope = slopes_ref[N_DIFF_HEADS + HEADS_PER_BLOCK * pl.program_id(1) + h]
        bias_ref[h] = logm_ref[...] + slope * ndist_ref[...]

    def scores(unit):
        i, h = unit
        n = (i + 1) * t
        q_pair = q_ref[0, i * t:n, :] * QK_SCALE
        return _dot_nt(_one_head(q_pair, h), k_ref[0, :n, :]) + bias_ref[h, :, s_len - n:]

    outs = {}

    def finish(unit, s):
        i, h = unit
        n = (i + 1) * t
        p, _ = _causal_probs(s, None, False)
        outs[h] = _normalise(_dot(p.astype(_BF16), values[h][:n]))
        if h == HEADS_PER_BLOCK - 1:
            o_ref[0, i * t:n, :] = jnp.where(first, outs[0], outs[1]).astype(o_ref.dtype)

    _pipelined([(i, h) for i in range(s_len // t) for h in range(HEADS_PER_BLOCK)], scores, finish,
               ATT_LOOKAHEAD)


def _also_casting(kernel, n_lead):
    def body(*refs):
        src, o_ref, dst = refs[n_lead:n_lead + 3]
        dst[...] = src[...].astype(dst.dtype)
        kernel(*refs[:n_lead], o_ref, *refs[n_lead + 3:])
    return body


def _attention_call(kernel, h3, extra_specs, extra_args, segs, scalars=(), scratch_shapes=(), cast=None):
    b, s, _ = h3.shape
    seg_q, seg_k, seg_v = segs
    n_groups = GROUP_WIDTH // LANES

    def col_block(seg):
        return pl.BlockSpec((1, s, LANES), lambda bb, g, *_: (bb, 0, seg + g))

    in_specs = [col_block(seg_q), col_block(seg_k), col_block(seg_v)] + list(extra_specs)
    out_specs = col_block(0)
    out_shape = jax.ShapeDtypeStruct((b, s, GROUP_WIDTH), _BF16)
    args = (*scalars, h3, h3, h3, *extra_args)
    if cast is not None:
        w, layer = cast
        _, rows, cols = w.shape
        slab = rows // (b * n_groups)
        kernel = _also_casting(kernel, len(args))
        in_specs.append(pl.BlockSpec((None, slab, cols), lambda bb, g, *_: (layer, bb * n_groups + g, 0)))
        out_specs = [out_specs, pl.BlockSpec((slab, cols), lambda bb, g, *_: (bb * n_groups + g, 0))]
        out_shape = [out_shape, jax.ShapeDtypeStruct((rows, cols), _BF16)]
        args = (*args, w)
    grid_spec = pltpu.PrefetchScalarGridSpec(
        num_scalar_prefetch=len(scalars),
        grid=(b, n_groups),
        in_specs=in_specs,
        out_specs=out_specs,
        scratch_shapes=list(scratch_shapes))
    return pl.pallas_call(
        kernel,
        grid_spec=grid_spec,
        out_shape=out_shape,
        compiler_params=_params("arbitrary", "arbitrary"),
    )(*args)


def _dilated_tables(s):
    r = lax.broadcasted_iota(jnp.int32, (ATT_T, s), 0)
    c = lax.broadcasted_iota(jnp.int32, (ATT_T, s), 1)
    dist = (s - ATT_T) + r - c
    mult = jnp.zeros(dist.shape, _F32)
    for window, dilation in DILATED_BRANCHES:
        hit = (dist >= 0) & (dist <= window) & (dist % dilation == 0)
        mult = mult + hit.astype(_F32)
    logm = jnp.where(mult > 0, jnp.log(jnp.maximum(mult, 1.0)), MASKED)
    return logm, -jnp.maximum(dist, 0).astype(_F32)


def _layer_norm(z, g, b):
    mu = jnp.mean(z, axis=-1, keepdims=True)
    zc = z - mu
    var = jnp.mean(zc * zc, axis=-1, keepdims=True)
    return zc * lax.rsqrt(var + LN_EPS) * g + b


def _out_proj_kernel(m0_ref, m1_ref, m2_ref, m3_ref, w_ref, x_ref, g_ref, b_ref, xo_ref, xob_ref):
    sub = OUT_PROJ_SUB_ROWS
    blocks = [slice(r, r + sub) for r in range(0, x_ref.shape[0], sub)]

    def project(rows):
        y = None
        for n, m_ref in enumerate((m0_ref, m1_ref, m2_ref, m3_ref)):
            part = _dot(m_ref[rows, :], w_ref[n * GROUP_WIDTH:(n + 1) * GROUP_WIDTH, :])
            y = part if y is None else y + part
        return y

    def finish(rows, y):
        out = _layer_norm(DEEPNORM_ALPHA * x_ref[rows, :] + y, g_ref[...], b_ref[...])
        xo_ref[rows, :] = out
        xob_ref[rows, :] = out.astype(_BF16)

    _pipelined(blocks, project, finish)


OUT_PROJ_SUB_ROWS = 128


def _out_projection(mixed, w, x, g, b, tm=512):
    m, d = x.shape
    row = lambda i: (i, 0)
    const = lambda i: (0, 0)
    return pl.pallas_call(
        _out_proj_kernel,
        grid=(m // tm,),
        in_specs=[pl.BlockSpec((tm, GROUP_WIDTH), row)] * 4 + [
            pl.BlockSpec((d, d), const), pl.BlockSpec((tm, d), row),
            pl.BlockSpec((1, d), const), pl.BlockSpec((1, d), const)],
        out_specs=[pl.BlockSpec((tm, d), row), pl.BlockSpec((tm, d), row)],
        out_shape=[jax.ShapeDtypeStruct((m, d), _F32), jax.ShapeDtypeStruct((m, d), _BF16)],
        compiler_params=_params("arbitrary"),
    )(*mixed, w, x, g, b)


def _mlp_kernel(xb_ref, x_ref, w1_hbm, w2_hbm, g_ref, b_ref, xo_ref, xob_ref,
                w1_buf, w2_buf, acc_ref, sems, *, tf):
    n_chunks = w1_hbm.shape[1] // tf
    assert n_chunks % 2 == 0 and n_chunks >= 2
    step, n_steps = pl.program_id(0), pl.num_programs(0)

    def w1_copy(c):
        return pltpu.make_async_copy(w1_hbm.at[:, pl.ds(c * tf, tf)], w1_buf.at[c % 2], sems.at[0, c % 2])

    def w2_copy(c):
        return pltpu.make_async_copy(w2_hbm.at[pl.ds(c * tf, tf), :], w2_buf.at[c % 2], sems.at[1, c % 2])

    def start(copy_of, c):
        if c < n_chunks:
            copy_of(c).start()
        else:
            pl.when(step + 1 < n_steps)(lambda: copy_of(c - n_chunks).start())

    @pl.when(step == 0)
    def _():
        w1_copy(0).start()
        w2_copy(0).start()
        w1_copy(1).start()

    w1_copy(0).wait()
    hidden = _dot(xb_ref[...], w1_buf[0])
    for c in range(n_chunks):
        start(w1_copy, c + 2)
        start(w2_copy, c + 1)
        if c + 1 < n_chunks:
            w1_copy(c + 1).wait()
        w2_copy(c).wait()
        ahead = _dot(xb_ref[...], w1_buf[(c + 1) % 2]) if c + 1 < n_chunks else None
        a = jnp.maximum(hidden, 0.0)
        part = _dot((a * a).astype(_BF16), w2_buf[c % 2])
        if c == 0:
            acc_ref[...] = part
        else:
            acc_ref[...] += part
        hidden = ahead

    out = _layer_norm(DEEPNORM_ALPHA * x_ref[...] + acc_ref[...], g_ref[...], b_ref[...])
    xo_ref[...] = out
    xob_ref[...] = out.astype(_BF16)


def _mlp(xb, x, w1, w2, g, b, tm=512, tf=1024):
    m, d = x.shape
    row = lambda i: (i, 0)
    const = lambda i: (0, 0)
    hbm = pl.BlockSpec(memory_space=pl.ANY)
    return pl.pallas_call(
        functools.partial(_mlp_kernel, tf=tf),
        grid=(m // tm,),
        in_specs=[pl.BlockSpec((tm, d), row), pl.BlockSpec((tm, d), row), hbm, hbm,
                  pl.BlockSpec((1, d), const), pl.BlockSpec((1, d), const)],
        out_specs=[pl.BlockSpec((tm, d), row), pl.BlockSpec((tm, d), row)],
        out_shape=[jax.ShapeDtypeStruct((m, d), _F32), jax.ShapeDtypeStruct((m, d), _BF16)],
        scratch_shapes=[pltpu.VMEM((2, d, tf), _BF16), pltpu.VMEM((2, tf, d), _BF16),
                        pltpu.VMEM((tm, d), _F32), pltpu.SemaphoreType.DMA((2, 2))],
        compiler_params=_params("arbitrary"),
    )(xb, x, w1, w2, g, b)


def _alibi_slopes():
    n = N_DIFF_HEADS + N_DIL_HEADS
    return jnp.asarray(np.exp2(-ALIBI_MAX_EXP * np.arange(1, n + 1) / n), _F32)


def _layer(x, xb, layer_idx, wt_in, b_fz, lam_vecs, subln_g, w_out, ln1_g, ln1_b,
           ln2_g, ln2_b, slopes, dil_tables, batch, casts_todo, mlp_weights):
    m, d = x.shape
    s = m // batch
    h3 = _in_projection(xb, wt_in, layer_idx).reshape(batch, s, -1)

    ct, w_out_b = _forget_cumsum(xb.reshape(batch, s, d), wt_in, b_fz, w_out, layer_idx)
    ct = ct.reshape(batch, N_FOX_HEADS, 1, s)

    def attention(kernel, *args, **kwargs):
        if not casts_todo:
            return _attention_call(kernel, h3, *args, **kwargs)
        o, w_b = _attention_call(kernel, h3, *args, cast=casts_todo.pop(0), **kwargs)
        mlp_weights.append(w_b)
        return o

    o_fox = attention(
        _fox_kernel,
        [pl.BlockSpec((1, HEADS_PER_BLOCK, 1, s), lambda b, g: (b, g, 0, 0))],
        (ct,), (_SEG["fq"], _SEG["fk"], _SEG["fv"]))

    o_sb = attention(_sb_kernel, [], (), (_SEG["sq"], _SEG["sk"], _SEG["sv"]))

    lam_init = 0.8 - 0.6 * math.exp(-0.3 * layer_idx)
    o_diff = attention(
        functools.partial(_diff_kernel, lam_init=lam_init),
        [pl.BlockSpec((4, HEAD_DIM), lambda b, g, *_: (0, 0)),
         pl.BlockSpec((1, 2 * HEAD_DIM), lambda b, g, *_: (0, 0))],
        (lam_vecs, subln_g), (_SEG["dq"], _SEG["dk"], _SEG["dv"]), scalars=(slopes,))

    o_dil = attention(
        _dil_kernel,
        [pl.BlockSpec((ATT_T, s), lambda b, g, *_: (0, 0))] * 2,
        dil_tables, (_SEG["gq"], _SEG["gk"], _SEG["gv"]), scalars=(slopes,),
        scratch_shapes=[pltpu.VMEM((HEADS_PER_BLOCK, ATT_T, s), _F32)])

    mixed = [o.reshape(m, GROUP_WIDTH) for o in (o_fox, o_sb, o_diff, o_dil)]
    x, xb = _out_projection(mixed, w_out_b, x, ln1_g, ln1_b)
    w1, w2 = mlp_weights[2 * layer_idx:2 * layer_idx + 2]
    return _mlp(xb, x, w1, w2, ln2_g, ln2_b)


def kernel(x, w_in, fox_forget_bias, diff_lambda_q1, diff_lambda_k1, diff_lambda_q2, diff_lambda_k2,
           diff_subln_gain, w_out, ln1_gain, ln1_bias, w_mlp_in, w_mlp_out, ln2_gain, ln2_bias):
    batch, s, d = x.shape
    assert d == D_MODEL and s % ATT_T == 0
    slopes = _alibi_slopes()
    dil_tables = _dilated_tables(s)
    xf = x.reshape(batch * s, d)
    xb = xf.astype(_BF16)
    wt_in = jnp.transpose(w_in, (0, 2, 1))
    casts_todo = [(w, l) for l in range(DEPTH) for w in (w_mlp_in, w_mlp_out)]
    mlp_weights = []
    for l in range(DEPTH):
        b_fz = fox_forget_bias[l].reshape(N_FOX_HEADS, 1)
        lam_vecs = jnp.stack([diff_lambda_q1[l], diff_lambda_k1[l], diff_lambda_q2[l], diff_lambda_k2[l]])
        xf, xb = _layer(
            xf, xb, l, wt_in, b_fz, lam_vecs, diff_subln_gain[l].reshape(1, -1),
            w_out, ln1_gain[l].reshape(1, d), ln1_bias[l].reshape(1, d),
            ln2_gain[l].reshape(1, d), ln2_bias[l].reshape(1, d), slopes, dil_tables, batch,
            casts_todo, mlp_weights)
    return xf.reshape(batch, s, d)
```

```python
import functools
import math

import numpy as np
import jax
import jax.numpy as jnp
from jax import lax
from jax.experimental import pallas as pl
from jax.experimental.pallas import tpu as pltpu

D_MODEL = 2048
DEPTH = 2
HEAD_DIM = 64
GROUP_WIDTH = 512
N_FOX_HEADS = 8
N_DIFF_HEADS = 4
N_DIL_HEADS = 8
D_FF = 4 * D_MODEL
DILATED_BRANCHES = ((128, 1), (512, 4), (2048, 16))
ALIBI_MAX_EXP = 8.0
LN_EPS = 1e-5
RMS_EPS = 1e-5
DEEPNORM_ALPHA = (2 * DEPTH) ** 0.25
QK_SCALE = HEAD_DIM ** -0.5

LANES = 128
HEADS_PER_BLOCK = LANES // HEAD_DIM
VMEM_LIMIT_BYTES = 56 * 1024 * 1024
MASKED = -1e30

_SEG = {name: 4 * i for i, name in enumerate(
    ("fq", "fk", "fv", "sq", "sk", "sv", "dq", "dk", "dv", "gq", "gk", "gv"))}

ATT_T = 256
ATT_LOOKAHEAD = 1

_F32 = jnp.float32
_BF16 = jnp.bfloat16


def _params(*sem):
    return pltpu.CompilerParams(dimension_semantics=sem, vmem_limit_bytes=VMEM_LIMIT_BYTES)


def _dot(a, b):
    return jnp.dot(a, b, preferred_element_type=_F32)


def _dot_nt(a, b):
    return lax.dot_general(a, b, (((1,), (1,)), ((), ())), preferred_element_type=_F32)


FORGET_ROW0 = 3 * GROUP_WIDTH


def _in_proj_kernel(x_ref, wt_ref, o_ref, wb_ref):
    @pl.when(pl.program_id(1) == 0)
    def _():
        wb_ref[...] = wt_ref[0].astype(_BF16)

    o_ref[...] = _dot_nt(x_ref[...], wb_ref[...]).astype(o_ref.dtype)


def _in_projection(xb, wt, layer, tm=2048, tn=768):
    m, k = xb.shape
    n = wt.shape[1] - N_FOX_HEADS
    assert FORGET_ROW0 % tn == 0

    def w_rows(j, i):
        groups = j * (tn // N_FOX_HEADS) + jnp.where(j * tn >= FORGET_ROW0, 1, 0)
        return (layer, groups * N_FOX_HEADS, 0)

    return pl.pallas_call(
        _in_proj_kernel,
        grid=(n // tn, m // tm),
        in_specs=[pl.BlockSpec((tm, k), lambda j, i: (i, 0)),
                  pl.BlockSpec((pl.Element(1), pl.Element(tn), pl.Element(k)), w_rows)],
        out_specs=pl.BlockSpec((tm, tn), lambda j, i: (i, j)),
        out_shape=jax.ShapeDtypeStruct((m, n), _BF16),
        scratch_shapes=[pltpu.VMEM((tn, k), _BF16)],
        compiler_params=_params("arbitrary", "arbitrary"),
    )(xb, wt)


def _split3(x):
    hi = x.astype(_BF16)
    r = x - hi.astype(_F32)
    mid = r.astype(_BF16)
    lo = (r - mid.astype(_F32)).astype(_BF16)
    return hi, mid, lo


def _log_sigmoid_pair(z):
    log_sig = jnp.minimum(z, 0.0) - jnp.log(1.0 + jnp.exp(-jnp.abs(z)))
    return log_sig, log_sig - z


def _forget_kernel(x_ref, w_ref, b_ref, wo_ref, ct_ref, wob_ref, *, blk):
    wob_ref[...] = wo_ref[...].astype(wob_ref.dtype)
    s = x_ref.shape[0]
    z = _dot_nt(w_ref[...].astype(_BF16), x_ref[...]) + b_ref[...]
    log_f, _ = _log_sigmoid_pair(z)
    r = lax.broadcasted_iota(jnp.int32, (blk, blk), 0)
    c = lax.broadcasted_iota(jnp.int32, (blk, blk), 1)
    upto = (r <= c).astype(_BF16)
    carry = jnp.zeros((N_FOX_HEADS, 1), _F32)
    for i in range(s // blk):
        hi, mid, lo = _split3(log_f[:, i * blk:(i + 1) * blk])
        cum = _dot(hi, upto) + _dot(mid, upto) + _dot(lo, upto) + carry
        ct_ref[0, :, i * blk:(i + 1) * blk] = cum
        carry = cum[:, blk - 1:blk]


def _forget_cumsum(xb, b, wt, b_fz, w_out, layer, blk=256):
    m, d = xb.shape
    s = m // b
    slab = w_out.shape[1] // b
    return pl.pallas_call(
        functools.partial(_forget_kernel, blk=blk),
        grid=(b,),
        in_specs=[pl.BlockSpec((s, d), lambda i: (i, 0)),
                  pl.BlockSpec((None, N_FOX_HEADS, d), lambda i: (layer, FORGET_ROW0 // N_FOX_HEADS, 0)),
                  pl.BlockSpec((N_FOX_HEADS, 1), lambda i: (0, 0)),
                  pl.BlockSpec((None, slab, w_out.shape[2]), lambda i: (layer, i, 0))],
        out_specs=[pl.BlockSpec((1, N_FOX_HEADS, s), lambda i: (i, 0, 0)),
                   pl.BlockSpec((slab, w_out.shape[2]), lambda i: (i, 0))],
        out_shape=[jax.ShapeDtypeStruct((b, N_FOX_HEADS, s), _F32),
                   jax.ShapeDtypeStruct(w_out.shape[1:], _BF16)],
        compiler_params=_params("arbitrary"),
    )(xb, wt, b_fz, w_out)


def _triangle(t):
    r = lax.broadcasted_iota(jnp.int32, (t, t), 0)
    c = lax.broadcasted_iota(jnp.int32, (t, t), 1)
    return r, c


def _head_lanes(h):
    lane = lax.broadcasted_iota(jnp.int32, (1, LANES), 1)
    return (lane >= h * HEAD_DIM) & (lane < (h + 1) * HEAD_DIM)


def _one_head(q_pair, h):
    return jnp.where(_head_lanes(h), q_pair, jnp.zeros_like(q_pair))


def _causal_probs(s, causal, with_sums):
    t, n = s.shape
    diag = s[:, n - t:] if causal is None else jnp.where(causal, s[:, n - t:], MASKED)
    m = jnp.max(diag, axis=-1, keepdims=True)
    if n > t:
        left = s[:, :n - t]
        m = jnp.maximum(m, jnp.max(left, axis=-1, keepdims=True))
    p_diag = jnp.exp(diag - m)
    l = jnp.sum(p_diag, axis=-1, keepdims=True) if with_sums else None
    if n == t:
        return p_diag, l
    p_left = jnp.exp(left - m)
    if with_sums:
        l = l + jnp.sum(p_left, axis=-1, keepdims=True)
    return jnp.concatenate([p_left, p_diag], axis=-1), l


def _values_with_ones(v_pair, h):
    return jnp.where(_head_lanes(h), v_pair, jnp.ones_like(v_pair))


def _normalise(o):
    return o / pltpu.roll(o, HEAD_DIM, 1)


def _pipelined(units, scores, finish, lookahead=1):
    pending = [scores(unit) for unit in units[:lookahead]]
    for u, unit in enumerate(units):
        if u + lookahead < len(units):
            pending.append(scores(units[u + lookahead]))
        finish(unit, pending.pop(0))


def _fox_kernel(q_ref, k_ref, v_ref, ct_ref, o_ref):
    t = ATT_T
    s_len = q_ref.shape[1]
    row, col = _triangle(t)
    causal = row >= col
    first = _head_lanes(0)
    values = [_values_with_ones(v_ref[0], h) for h in range(HEADS_PER_BLOCK)]

    def scores(unit):
        i, h = unit
        n = (i + 1) * t
        c_row = ct_ref[0, h, :, :n]
        key_bias = c_row[:, n - 1:n] - c_row
        q_pair = q_ref[0, i * t:n, :] * QK_SCALE
        return _dot_nt(_one_head(q_pair, h), k_ref[0, :n, :]) + key_bias

    outs = {}

    def finish(unit, s):
        i, h = unit
        n = (i + 1) * t
        p, _ = _causal_probs(s, causal, False)
        outs[h] = _normalise(_dot(p.astype(_BF16), values[h][:n]))
        if h == HEADS_PER_BLOCK - 1:
            o_ref[0, i * t:n, :] = jnp.where(first, outs[0], outs[1]).astype(o_ref.dtype)

    _pipelined([(i, h) for i in range(s_len // t) for h in range(HEADS_PER_BLOCK)], scores, finish,
               ATT_LOOKAHEAD)


def _sb_kernel(q_ref, k_ref, v_ref, o_ref):
    t = ATT_T
    s_len = q_ref.shape[1]
    row, col = _triangle(t)
    strict = row > col
    from_ones = (row >= col).astype(_BF16)
    from_ones2 = jnp.concatenate([from_ones, from_ones], axis=0)
    first = _head_lanes(0)

    def scores(unit):
        i, h = unit
        q_pair = q_ref[0, i * t:(i + 1) * t, :] * QK_SCALE
        return _dot_nt(_one_head(q_pair, h), k_ref[0, :(i + 1) * t, :])

    outs = {}

    def finish(unit, z):
        i, h = unit
        n = (i + 1) * t
        nz = -z
        log_rest = jnp.minimum(nz, 0.0) - jnp.log(1.0 + jnp.exp(jnp.minimum(z, nz)))
        incl, sums = [], []
        for c in range(i + 1):
            lr = log_rest[:, c * t:(c + 1) * t]
            if c == i:
                lr = jnp.where(strict, lr, 0.0)
            hi = lr.astype(_BF16)
            lo = (lr - hi.astype(_F32)).astype(_BF16)
            incl.append(_dot(jnp.concatenate([hi, lo], axis=-1), from_ones2))
            sums.append(incl[c][:, :1])
        tail = jnp.zeros((t, 1), _F32)
        chunks = [None] * (i + 1)
        for c in range(i, -1, -1):
            a = jnp.exp(z[:, c * t:(c + 1) * t] + incl[c] + tail)
            if c == i:
                a = jnp.where(strict, a, 0.0)
            chunks[c] = a.astype(_BF16)
            tail = tail + sums[c]
        a_all = chunks[0] if i == 0 else jnp.concatenate(chunks, axis=-1)
        outs[h] = _dot(a_all, v_ref[0, :n, :])
        if h == HEADS_PER_BLOCK - 1:
            o_ref[0, i * t:n, :] = jnp.where(first, outs[0], outs[1]).astype(o_ref.dtype)

    _pipelined([(i, h) for i in range(s_len // t) for h in range(HEADS_PER_BLOCK)], scores, finish,
               ATT_LOOKAHEAD)


def _diff_kernel(slopes_ref, q_ref, k_ref, v_ref, lam_ref, g_ref, o_ref, *, lam_init):
    t = ATT_T
    s_len = q_ref.shape[1]
    slope = slopes_ref[pl.program_id(1)]
    row, col = _triangle(t)
    causal = row >= col
    lq1, lk1, lq2, lk2 = (lam_ref[n:n + 1, :] for n in range(4))
    lam = (jnp.exp(jnp.sum(lq1 * lk1, axis=-1, keepdims=True))
           - jnp.exp(jnp.sum(lq2 * lk2, axis=-1, keepdims=True)) + lam_init)
    gain = g_ref[...] * (1.0 - lam_init)

    def scores(unit):
        i, h = unit
        n = (i + 1) * t
        q_pair = q_ref[0, i * t:n, :] * QK_SCALE
        key_bias = slope * (lax.broadcasted_iota(jnp.int32, (1, n), 1) - (n - 1)).astype(_F32)
        return _dot_nt(_one_head(q_pair, h), k_ref[0, :n, :]) + key_bias

    values = jnp.concatenate([v_ref[0], jnp.ones_like(v_ref[0])], axis=-1)
    maps = {}

    def finish(unit, s):
        i, h = unit
        n = (i + 1) * t
        p, _ = _causal_probs(s, causal, False)
        o = _dot(p.astype(_BF16), values[:n])
        maps[h] = o[:, :LANES] / o[:, LANES:]
        if h == 1:
            od = maps[0] - lam * maps[1]
            od = od * lax.rsqrt(jnp.mean(od * od, axis=-1, keepdims=True) + RMS_EPS)
            o_ref[0, i * t:n, :] = (od * gain).astype(o_ref.dtype)

    _pipelined([(i, h) for i in range(s_len // t) for h in range(2)], scores, finish, ATT_LOOKAHEAD)


def _dil_kernel(slopes_ref, q_ref, k_ref, v_ref, logm_ref, ndist_ref, o_ref, bias_ref):
    t = ATT_T
    s_len = q_ref.shape[1]
    first = _head_lanes(0)
    values = [_values_with_ones(v_ref[0], h) for h in range(HEADS_PER_BLOCK)]
    for h in range(HEADS_PER_BLOCK):
        slope = slopes_ref[N_DIFF_HEADS + HEADS_PER_BLOCK * pl.program_id(1) + h]
        bias_ref[h] = logm_ref[...] + slope * ndist_ref[...]

    def scores(unit):
        i, h = unit
        n = (i + 1) * t
        q_pair = q_ref[0, i * t:n, :] * QK_SCALE
        return _dot_nt(_one_head(q_pair, h), k_ref[0, :n, :]) + bias_ref[h, :, s_len - n:]

    outs = {}

    def finish(unit, s):
        i, h = unit
        n = (i + 1) * t
        p, _ = _causal_probs(s, None, False)
        outs[h] = _normalise(_dot(p.astype(_BF16), values[h][:n]))
        if h == HEADS_PER_BLOCK - 1:
            o_ref[0, i * t:n, :] = jnp.where(first, outs[0], outs[1]).astype(o_ref.dtype)

    _pipelined([(i, h) for i in range(s_len // t) for h in range(HEADS_PER_BLOCK)], scores, finish,
               ATT_LOOKAHEAD)


def _also_casting(kernel, n_lead):
    def body(*refs):
        src, o_ref, dst = refs[n_lead:n_lead + 3]
        dst[...] = src[...].astype(dst.dtype)
        kernel(*refs[:n_lead], o_ref, *refs[n_lead + 3:])
    return body


def _attention_call(kernel, h3, extra_specs, extra_args, segs, scalars=(), scratch_shapes=(), cast=None):
    b, s, _ = h3.shape
    seg_q, seg_k, seg_v = segs
    n_groups = GROUP_WIDTH // LANES

    def col_block(seg):
        return pl.BlockSpec((1, s, LANES), lambda bb, g, *_: (bb, 0, seg + g))

    in_specs = [col_block(seg_q), col_block(seg_k), col_block(seg_v)] + list(extra_specs)
    out_specs = col_block(0)
    out_shape = jax.ShapeDtypeStruct((b, s, GROUP_WIDTH), _BF16)
    args = (*scalars, h3, h3, h3, *extra_args)
    if cast is not None:
        w, layer = cast
        _, rows, cols = w.shape
        slab = rows // (b * n_groups)
        kernel = _also_casting(kernel, len(args))
        in_specs.append(pl.BlockSpec((None, slab, cols), lambda bb, g, *_: (layer, bb * n_groups + g, 0)))
        out_specs = [out_specs, pl.BlockSpec((slab, cols), lambda bb, g, *_: (bb * n_groups + g, 0))]
        out_shape = [out_shape, jax.ShapeDtypeStruct((rows, cols), _BF16)]
        args = (*args, w)
    grid_spec = pltpu.PrefetchScalarGridSpec(
        num_scalar_prefetch=len(scalars),
        grid=(b, n_groups),
        in_specs=in_specs,
        out_specs=out_specs,
        scratch_shapes=list(scratch_shapes))
    return pl.pallas_call(
        kernel,
        grid_spec=grid_spec,
        out_shape=out_shape,
        compiler_params=_params("arbitrary", "arbitrary"),
    )(*args)


def _dilated_tables(s):
    r = lax.broadcasted_iota(jnp.int32, (ATT_T, s), 0)
    c = lax.broadcasted_iota(jnp.int32, (ATT_T, s), 1)
    dist = (s - ATT_T) + r - c
    mult = jnp.zeros(dist.shape, _F32)
    for window, dilation in DILATED_BRANCHES:
        hit = (dist >= 0) & (dist <= window) & (dist % dilation == 0)
        mult = mult + hit.astype(_F32)
    logm = jnp.where(mult > 0, jnp.log(jnp.maximum(mult, 1.0)), MASKED)
    return logm, -jnp.maximum(dist, 0).astype(_F32)


def _layer_norm(z, g, b):
    mu = jnp.mean(z, axis=-1, keepdims=True)
    zc = z - mu
    var = jnp.mean(zc * zc, axis=-1, keepdims=True)
    return zc * lax.rsqrt(var + LN_EPS) * g + b


def _out_proj_kernel(m0_ref, m1_ref, m2_ref, m3_ref, w_ref, x_ref, g_ref, b_ref, xo_ref, xob_ref):
    sub = OUT_PROJ_SUB_ROWS
    blocks = [slice(r, r + sub) for r in range(0, x_ref.shape[0], sub)]

    def project(rows):
        y = None
        for n, m_ref in enumerate((m0_ref, m1_ref, m2_ref, m3_ref)):
            part = _dot(m_ref[rows, :], w_ref[n * GROUP_WIDTH:(n + 1) * GROUP_WIDTH, :])
            y = part if y is None else y + part
        return y

    def finish(rows, y):
        out = _layer_norm(DEEPNORM_ALPHA * x_ref[rows, :] + y, g_ref[...], b_ref[...])
        xo_ref[rows, :] = out
        xob_ref[rows, :] = out.astype(_BF16)

    _pipelined(blocks, project, finish)


OUT_PROJ_SUB_ROWS = 128


def _out_projection(mixed, w, x, g, b, tm=512):
    m, d = x.shape
    row = lambda i: (i, 0)
    const = lambda i: (0, 0)
    return pl.pallas_call(
        _out_proj_kernel,
        grid=(m // tm,),
        in_specs=[pl.BlockSpec((tm, GROUP_WIDTH), row)] * 4 + [
            pl.BlockSpec((d, d), const), pl.BlockSpec((tm, d), row),
            pl.BlockSpec((1, d), const), pl.BlockSpec((1, d), const)],
        out_specs=[pl.BlockSpec((tm, d), row), pl.BlockSpec((tm, d), row)],
        out_shape=[jax.ShapeDtypeStruct((m, d), _F32), jax.ShapeDtypeStruct((m, d), _BF16)],
        compiler_params=_params("arbitrary"),
    )(*mixed, w, x, g, b)


def _mlp_kernel(xb_ref, x_ref, w1_hbm, w2_hbm, g_ref, b_ref, xo_ref, xob_ref,
                w1_buf, w2_buf, acc_ref, sems, *, tf):
    n_chunks = w1_hbm.shape[1] // tf
    assert n_chunks % 2 == 0 and n_chunks >= 2
    step, n_steps = pl.program_id(0), pl.num_programs(0)

    def w1_copy(c):
        return pltpu.make_async_copy(w1_hbm.at[:, pl.ds(c * tf, tf)], w1_buf.at[c % 2], sems.at[0, c % 2])

    def w2_copy(c):
        return pltpu.make_async_copy(w2_hbm.at[pl.ds(c * tf, tf), :], w2_buf.at[c % 2], sems.at[1, c % 2])

    def start(copy_of, c):
        if c < n_chunks:
            copy_of(c).start()
        else:
            pl.when(step + 1 < n_steps)(lambda: copy_of(c - n_chunks).start())

    @pl.when(step == 0)
    def _():
        w1_copy(0).start()
        w2_copy(0).start()
        w1_copy(1).start()

    w1_copy(0).wait()
    hidden = _dot(xb_ref[...], w1_buf[0])
    for c in range(n_chunks):
        start(w1_copy, c + 2)
        start(w2_copy, c + 1)
        if c + 1 < n_chunks:
            w1_copy(c + 1).wait()
        w2_copy(c).wait()
        ahead = _dot(xb_ref[...], w1_buf[(c + 1) % 2]) if c + 1 < n_chunks else None
        a = jnp.maximum(hidden, 0.0)
        part = _dot((a * a).astype(_BF16), w2_buf[c % 2])
        if c == 0:
            acc_ref[...] = part
        else:
            acc_ref[...] += part
        hidden = ahead

    out = _layer_norm(DEEPNORM_ALPHA * x_ref[...] + acc_ref[...], g_ref[...], b_ref[...])
    xo_ref[...] = out
    xob_ref[...] = out.astype(_BF16)


def _mlp(xb, x, w1, w2, g, b, tm=512, tf=1024):
    m, d = x.shape
    row = lambda i: (i, 0)
    const = lambda i: (0, 0)
    hbm = pl.BlockSpec(memory_space=pl.ANY)
    return pl.pallas_call(
        functools.partial(_mlp_kernel, tf=tf),
        grid=(m // tm,),
        in_specs=[pl.BlockSpec((tm, d), row), pl.BlockSpec((tm, d), row), hbm, hbm,
                  pl.BlockSpec((1, d), const), pl.BlockSpec((1, d), const)],
        out_specs=[pl.BlockSpec((tm, d), row), pl.BlockSpec((tm, d), row)],
        out_shape=[jax.ShapeDtypeStruct((m, d), _F32), jax.ShapeDtypeStruct((m, d), _BF16)],
        scratch_shapes=[pltpu.VMEM((2, d, tf), _BF16), pltpu.VMEM((2, tf, d), _BF16),
                        pltpu.VMEM((tm, d), _F32), pltpu.SemaphoreType.DMA((2, 2))],
        compiler_params=_params("arbitrary"),
    )(xb, x, w1, w2, g, b)


def _alibi_slopes():
    n = N_DIFF_HEADS + N_DIL_HEADS
    return jnp.asarray(np.exp2(-ALIBI_MAX_EXP * np.arange(1, n + 1) / n), _F32)


def _layer(x, xb, layer_idx, wt_in, b_fz, lam_vecs, subln_g, w_out, ln1_g, ln1_b,
           ln2_g, ln2_b, slopes, dil_tables, batch, casts_todo, mlp_weights):
    m, d = x.shape
    s = m // batch
    h3 = _in_projection(xb, wt_in, layer_idx).reshape(batch, s, -1)

    ct, w_out_b = _forget_cumsum(xb, batch, wt_in, b_fz, w_out, layer_idx)
    ct = ct.reshape(batch, N_FOX_HEADS, 1, s)

    def attention(kernel, *args, **kwargs):
        if not casts_todo:
            return _attention_call(kernel, h3, *args, **kwargs)
        o, w_b = _attention_call(kernel, h3, *args, cast=casts_todo.pop(0), **kwargs)
        mlp_weights.append(w_b)
        return o

    o_fox = attention(
        _fox_kernel,
        [pl.BlockSpec((1, HEADS_PER_BLOCK, 1, s), lambda b, g: (b, g, 0, 0))],
        (ct,), (_SEG["fq"], _SEG["fk"], _SEG["fv"]))

    o_sb = attention(_sb_kernel, [], (), (_SEG["sq"], _SEG["sk"], _SEG["sv"]))

    lam_init = 0.8 - 0.6 * math.exp(-0.3 * layer_idx)
    o_diff = attention(
        functools.partial(_diff_kernel, lam_init=lam_init),
        [pl.BlockSpec((4, HEAD_DIM), lambda b, g, *_: (0, 0)),
         pl.BlockSpec((1, 2 * HEAD_DIM), lambda b, g, *_: (0, 0))],
        (lam_vecs, subln_g), (_SEG["dq"], _SEG["dk"], _SEG["dv"]), scalars=(slopes,))

    o_dil = attention(
        _dil_kernel,
        [pl.BlockSpec((ATT_T, s), lambda b, g, *_: (0, 0))] * 2,
        dil_tables, (_SEG["gq"], _SEG["gk"], _SEG["gv"]), scalars=(slopes,),
        scratch_shapes=[pltpu.VMEM((HEADS_PER_BLOCK, ATT_T, s), _F32)])

    mixed = [o.reshape(m, GROUP_WIDTH) for o in (o_fox, o_sb, o_diff, o_dil)]
    x, xb = _out_projection(mixed, w_out_b, x, ln1_g, ln1_b)
    w1, w2 = mlp_weights[2 * layer_idx:2 * layer_idx + 2]
    return _mlp(xb, x, w1, w2, ln2_g, ln2_b)


def kernel(x, w_in, fox_forget_bias, diff_lambda_q1, diff_lambda_k1, diff_lambda_q2, diff_lambda_k2,
           diff_subln_gain, w_out, ln1_gain, ln1_bias, w_mlp_in, w_mlp_out, ln2_gain, ln2_bias):
    batch, s, d = x.shape
    assert d == D_MODEL and s % ATT_T == 0
    slopes = _alibi_slopes()
    dil_tables = _dilated_tables(s)
    xf = x.reshape(batch * s, d)
    xb = xf.astype(_BF16)
    wt_in = jnp.transpose(w_in, (0, 2, 1))
    casts_todo = [(w, l) for l in range(DEPTH) for w in (w_mlp_in, w_mlp_out)]
    mlp_weights = []
    for l in range(DEPTH):
        b_fz = fox_forget_bias[l].reshape(N_FOX_HEADS, 1)
        lam_vecs = jnp.stack([diff_lambda_q1[l], diff_lambda_k1[l], diff_lambda_q2[l], diff_lambda_k2[l]])
        xf, xb = _layer(
            xf, xb, l, wt_in, b_fz, lam_vecs, diff_subln_gain[l].reshape(1, -1),
            w_out, ln1_gain[l].reshape(1, d), ln1_bias[l].reshape(1, d),
            ln2_gain[l].reshape(1, d), ln2_bias[l].reshape(1, d), slopes, dil_tables, batch,
            casts_todo, mlp_weights)
    return xf.reshape(batch, s, d)
```

```python
import functools
import math

import numpy as np
import jax
import jax.numpy as jnp
from jax import lax
from jax.experimental import pallas as pl
from jax.experimental.pallas import tpu as pltpu

D_MODEL = 2048
DEPTH = 2
HEAD_DIM = 64
GROUP_WIDTH = 512
N_FOX_HEADS = 8
N_DIFF_HEADS = 4
N_DIL_HEADS = 8
DILATED_BRANCHES = ((128, 1), (512, 4), (2048, 16))
ALIBI_MAX_EXP = 8.0
LN_EPS = 1e-5
RMS_EPS = 1e-5
DEEPNORM_ALPHA = (2 * DEPTH) ** 0.25
QK_SCALE = HEAD_DIM ** -0.5

LANES = 128
HEADS_PER_BLOCK = LANES // HEAD_DIM
VMEM_LIMIT_BYTES = 56 * 1024 * 1024
MASKED = -1e30

_SEG = {name: (GROUP_WIDTH // LANES) * i for i, name in enumerate(
    ("fq", "fk", "fv", "sq", "sk", "sv", "dq", "dk", "dv", "gq", "gk", "gv"))}

ATT_T = 256

_F32 = jnp.float32
_BF16 = jnp.bfloat16


def _params(*sem):
    return pltpu.CompilerParams(dimension_semantics=sem, vmem_limit_bytes=VMEM_LIMIT_BYTES)


def _dot(a, b):
    return jnp.dot(a, b, preferred_element_type=_F32)


def _dot_nt(a, b):
    return lax.dot_general(a, b, (((1,), (1,)), ((), ())), preferred_element_type=_F32)


FORGET_ROW0 = 3 * GROUP_WIDTH


def _in_proj_kernel(x_ref, wt_ref, o_ref, wb_ref):
    @pl.when(pl.program_id(1) == 0)
    def _():
        wb_ref[...] = wt_ref[0].astype(_BF16)

    o_ref[...] = _dot_nt(x_ref[...], wb_ref[...]).astype(o_ref.dtype)


def _in_projection(xb, wt, layer, tm=2048, tn=768):
    m, k = xb.shape
    n = wt.shape[1] - N_FOX_HEADS
    assert FORGET_ROW0 % tn == 0

    def w_rows(j, i):
        groups = j * (tn // N_FOX_HEADS) + jnp.where(j * tn >= FORGET_ROW0, 1, 0)
        return (layer, groups * N_FOX_HEADS, 0)

    return pl.pallas_call(
        _in_proj_kernel,
        grid=(n // tn, m // tm),
        in_specs=[pl.BlockSpec((tm, k), lambda j, i: (i, 0)),
                  pl.BlockSpec((pl.Element(1), pl.Element(tn), pl.Element(k)), w_rows)],
        out_specs=pl.BlockSpec((tm, tn), lambda j, i: (i, j)),
        out_shape=jax.ShapeDtypeStruct((m, n), _BF16),
        scratch_shapes=[pltpu.VMEM((tn, k), _BF16)],
        compiler_params=_params("arbitrary", "arbitrary"),
    )(xb, wt)


def _split3(x):
    hi = x.astype(_BF16)
    r = x - hi.astype(_F32)
    mid = r.astype(_BF16)
    lo = (r - mid.astype(_F32)).astype(_BF16)
    return hi, mid, lo


def _log_sigmoid(z):
    return jnp.minimum(z, 0.0) - jnp.log(1.0 + jnp.exp(-jnp.abs(z)))


def _forget_kernel(x_ref, w_ref, b_ref, wo_ref, ct_ref, wob_ref, *, blk):
    wob_ref[...] = wo_ref[...].astype(wob_ref.dtype)
    s = x_ref.shape[0]
    z = _dot_nt(w_ref[...].astype(_BF16), x_ref[...]) + b_ref[...]
    log_f = _log_sigmoid(z)
    r = lax.broadcasted_iota(jnp.int32, (blk, blk), 0)
    c = lax.broadcasted_iota(jnp.int32, (blk, blk), 1)
    upto = (r <= c).astype(_BF16)
    carry = jnp.zeros((N_FOX_HEADS, 1), _F32)
    for i in range(s // blk):
        hi, mid, lo = _split3(log_f[:, i * blk:(i + 1) * blk])
        cum = _dot(hi, upto) + _dot(mid, upto) + _dot(lo, upto) + carry
        ct_ref[0, :, i * blk:(i + 1) * blk] = cum
        carry = cum[:, blk - 1:blk]


def _forget_cumsum(xb, b, wt, b_fz, w_out, layer, blk=256):
    m, d = xb.shape
    s = m // b
    slab = w_out.shape[1] // b
    return pl.pallas_call(
        functools.partial(_forget_kernel, blk=blk),
        grid=(b,),
        in_specs=[pl.BlockSpec((s, d), lambda i: (i, 0)),
                  pl.BlockSpec((None, N_FOX_HEADS, d), lambda i: (layer, FORGET_ROW0 // N_FOX_HEADS, 0)),
                  pl.BlockSpec((N_FOX_HEADS, 1), lambda i: (0, 0)),
                  pl.BlockSpec((None, slab, w_out.shape[2]), lambda i: (layer, i, 0))],
        out_specs=[pl.BlockSpec((1, N_FOX_HEADS, s), lambda i: (i, 0, 0)),
                   pl.BlockSpec((slab, w_out.shape[2]), lambda i: (i, 0))],
        out_shape=[jax.ShapeDtypeStruct((b, N_FOX_HEADS, s), _F32),
                   jax.ShapeDtypeStruct(w_out.shape[1:], _BF16)],
        compiler_params=_params("arbitrary"),
    )(xb, wt, b_fz, w_out)


def _triangle(t):
    r = lax.broadcasted_iota(jnp.int32, (t, t), 0)
    c = lax.broadcasted_iota(jnp.int32, (t, t), 1)
    return r, c


def _head_lanes(h):
    lane = lax.broadcasted_iota(jnp.int32, (1, LANES), 1)
    return (lane >= h * HEAD_DIM) & (lane < (h + 1) * HEAD_DIM)


def _one_head(q_pair, h):
    return jnp.where(_head_lanes(h), q_pair, jnp.zeros_like(q_pair))


def _masked_with_max(s, causal):
    t, n = s.shape
    if causal is not None:
        diag = jnp.where(causal, s[:, n - t:], MASKED)
        s = diag if n == t else jnp.concatenate([s[:, :n - t], diag], axis=-1)
    return s, jnp.max(s, axis=-1, keepdims=True)


def _values_with_ones(v_pair, h):
    return jnp.where(_head_lanes(h), v_pair, jnp.ones_like(v_pair))


def _normalise(o):
    return o / pltpu.roll(o, HEAD_DIM, 1)


def _pipelined(units, *stages):
    carried = [{} for _ in stages]
    for step in range(len(units) + len(stages) - 1):
        for k, stage in enumerate(stages):
            u = step - k
            if 0 <= u < len(units):
                args = (units[u], carried[k - 1].pop(u)) if k else (units[u],)
                carried[k][u] = stage(*args)


def _fox_kernel(q_ref, k_ref, v_ref, ct_ref, o_ref):
    t = ATT_T
    s_len = q_ref.shape[1]
    row, col = _triangle(t)
    causal = row >= col
    first = _head_lanes(0)
    values = [_values_with_ones(v_ref[0], h) for h in range(HEADS_PER_BLOCK)]

    def scores(unit):
        i, h = unit
        n = (i + 1) * t
        c_row = ct_ref[0, h, :, :n]
        key_bias = c_row[:, n - 1:n] - c_row
        q_pair = q_ref[0, i * t:n, :] * QK_SCALE
        return _masked_with_max(_dot_nt(_one_head(q_pair, h), k_ref[0, :n, :]) + key_bias, causal)

    outs = {}

    def finish(unit, logits):
        i, h = unit
        n = (i + 1) * t
        s, m = logits
        outs[h] = _normalise(_dot(jnp.exp(s - m).astype(_BF16), values[h][:n]))
        if h == HEADS_PER_BLOCK - 1:
            o_ref[0, i * t:n, :] = jnp.where(first, outs[0], outs[1]).astype(o_ref.dtype)

    _pipelined([(i, h) for i in range(s_len // t) for h in range(HEADS_PER_BLOCK)], scores, finish)


def _sb_kernel(q_ref, k_ref, v_ref, o_ref):
    t = ATT_T
    s_len = q_ref.shape[1]
    row, col = _triangle(t)
    strict = row > col
    from_ones = (row >= col).astype(_BF16)
    from_ones2 = jnp.concatenate([from_ones, from_ones], axis=0)
    first = _head_lanes(0)

    def scores(unit):
        i, h = unit
        q_pair = q_ref[0, i * t:(i + 1) * t, :] * QK_SCALE
        return _dot_nt(_one_head(q_pair, h), k_ref[0, :(i + 1) * t, :])

    outs = {}

    def finish(unit, z):
        i, h = unit
        n = (i + 1) * t
        nz = -z
        log_rest = jnp.minimum(nz, 0.0) - jnp.log(1.0 + jnp.exp(jnp.minimum(z, nz)))
        incl, sums = [], []
        for c in range(i + 1):
            lr = log_rest[:, c * t:(c + 1) * t]
            if c == i:
                lr = jnp.where(strict, lr, 0.0)
            hi = lr.astype(_BF16)
            lo = (lr - hi.astype(_F32)).astype(_BF16)
            incl.append(_dot(jnp.concatenate([hi, lo], axis=-1), from_ones2))
            sums.append(incl[c][:, :1])
        tail = jnp.zeros((t, 1), _F32)
        chunks = [None] * (i + 1)
        for c in range(i, -1, -1):
            a = jnp.exp(z[:, c * t:(c + 1) * t] + incl[c] + tail)
            if c == i:
                a = jnp.where(strict, a, 0.0)
            chunks[c] = a.astype(_BF16)
            tail = tail + sums[c]
        a_all = chunks[0] if i == 0 else jnp.concatenate(chunks, axis=-1)
        outs[h] = _dot(a_all, v_ref[0, :n, :])
        if h == HEADS_PER_BLOCK - 1:
            o_ref[0, i * t:n, :] = jnp.where(first, outs[0], outs[1]).astype(o_ref.dtype)

    _pipelined([(i, h) for i in range(s_len // t) for h in range(HEADS_PER_BLOCK)], scores, finish)


def _diff_kernel(slopes_ref, q_ref, k_ref, v_ref, lam_ref, g_ref, o_ref, *, lam_init):
    t = ATT_T
    s_len = q_ref.shape[1]
    slope = slopes_ref[pl.program_id(1)]
    row, col = _triangle(t)
    causal = row >= col
    lq1, lk1, lq2, lk2 = (lam_ref[n:n + 1, :] for n in range(4))
    lam = (jnp.exp(jnp.sum(lq1 * lk1, axis=-1, keepdims=True))
           - jnp.exp(jnp.sum(lq2 * lk2, axis=-1, keepdims=True)) + lam_init)
    gain = g_ref[...] * (1.0 - lam_init)

    def scores(unit):
        i, h = unit
        n = (i + 1) * t
        q_pair = q_ref[0, i * t:n, :] * QK_SCALE
        key_bias = slope * (lax.broadcasted_iota(jnp.int32, (1, n), 1) - (n - 1)).astype(_F32)
        return _masked_with_max(_dot_nt(_one_head(q_pair, h), k_ref[0, :n, :]) + key_bias, causal)

    values = jnp.concatenate([v_ref[0], jnp.ones_like(v_ref[0])], axis=-1)
    maps = {}

    def finish(unit, logits):
        i, h = unit
        n = (i + 1) * t
        s, m = logits
        o = _dot(jnp.exp(s - m).astype(_BF16), values[:n])
        maps[h] = o[:, :LANES] / o[:, LANES:]
        if h == 1:
            od = maps[0] - lam * maps[1]
            od = od * lax.rsqrt(jnp.mean(od * od, axis=-1, keepdims=True) + RMS_EPS)
            o_ref[0, i * t:n, :] = (od * gain).astype(o_ref.dtype)

    _pipelined([(i, h) for i in range(s_len // t) for h in range(2)], scores, finish)


def _dil_kernel(slopes_ref, q_ref, k_ref, v_ref, logm_ref, ndist_ref, o_ref, bias_ref):
    t = ATT_T
    s_len = q_ref.shape[1]
    first = _head_lanes(0)
    values = [_values_with_ones(v_ref[0], h) for h in range(HEADS_PER_BLOCK)]
    for h in range(HEADS_PER_BLOCK):
        slope = slopes_ref[N_DIFF_HEADS + HEADS_PER_BLOCK * pl.program_id(1) + h]
        bias_ref[h] = logm_ref[...] + slope * ndist_ref[...]

    def scores(unit):
        i, h = unit
        n = (i + 1) * t
        q_pair = q_ref[0, i * t:n, :] * QK_SCALE
        s = _dot_nt(_one_head(q_pair, h), k_ref[0, :n, :]) + bias_ref[h, :, s_len - n:]
        return _masked_with_max(s, None)

    outs = {}

    def finish(unit, logits):
        i, h = unit
        n = (i + 1) * t
        s, m = logits
        outs[h] = _normalise(_dot(jnp.exp(s - m).astype(_BF16), values[h][:n]))
        if h == HEADS_PER_BLOCK - 1:
            o_ref[0, i * t:n, :] = jnp.where(first, outs[0], outs[1]).astype(o_ref.dtype)

    _pipelined([(i, h) for i in range(s_len // t) for h in range(HEADS_PER_BLOCK)], scores, finish)


def _also_casting(kernel, n_lead):
    def body(*refs):
        src, o_ref, dst = refs[n_lead:n_lead + 3]
        dst[...] = src[...].astype(dst.dtype)
        kernel(*refs[:n_lead], o_ref, *refs[n_lead + 3:])
    return body


def _attention_call(kernel, h3, extra_specs, extra_args, segs, scalars=(), scratch_shapes=(), cast=None):
    b, s, _ = h3.shape
    seg_q, seg_k, seg_v = segs
    n_groups = GROUP_WIDTH // LANES

    def col_block(seg):
        return pl.BlockSpec((1, s, LANES), lambda bb, g, *_: (bb, 0, seg + g))

    in_specs = [col_block(seg_q), col_block(seg_k), col_block(seg_v)] + list(extra_specs)
    out_specs = col_block(0)
    out_shape = jax.ShapeDtypeStruct((b, s, GROUP_WIDTH), _BF16)
    args = (*scalars, h3, h3, h3, *extra_args)
    if cast is not None:
        w, layer = cast
        _, rows, cols = w.shape
        slab = rows // (b * n_groups)
        kernel = _also_casting(kernel, len(args))
        in_specs.append(pl.BlockSpec((None, slab, cols), lambda bb, g, *_: (layer, bb * n_groups + g, 0)))
        out_specs = [out_specs, pl.BlockSpec((slab, cols), lambda bb, g, *_: (bb * n_groups + g, 0))]
        out_shape = [out_shape, jax.ShapeDtypeStruct((rows, cols), _BF16)]
        args = (*args, w)
    grid_spec = pltpu.PrefetchScalarGridSpec(
        num_scalar_prefetch=len(scalars),
        grid=(b, n_groups),
        in_specs=in_specs,
        out_specs=out_specs,
        scratch_shapes=list(scratch_shapes))
    return pl.pallas_call(
        kernel,
        grid_spec=grid_spec,
        out_shape=out_shape,
        compiler_params=_params("arbitrary", "arbitrary"),
    )(*args)


def _dilated_tables(s):
    r = lax.broadcasted_iota(jnp.int32, (ATT_T, s), 0)
    c = lax.broadcasted_iota(jnp.int32, (ATT_T, s), 1)
    dist = (s - ATT_T) + r - c
    mult = jnp.zeros(dist.shape, _F32)
    for window, dilation in DILATED_BRANCHES:
        hit = (dist >= 0) & (dist <= window) & (dist % dilation == 0)
        mult = mult + hit.astype(_F32)
    logm = jnp.where(mult > 0, jnp.log(jnp.maximum(mult, 1.0)), MASKED)
    return logm, -jnp.maximum(dist, 0).astype(_F32)


def _layer_norm(z, g, b):
    mu = jnp.mean(z, axis=-1, keepdims=True)
    zc = z - mu
    var = jnp.mean(zc * zc, axis=-1, keepdims=True)
    return zc * lax.rsqrt(var + LN_EPS) * g + b


def _out_proj_kernel(m0_ref, m1_ref, m2_ref, m3_ref, w_ref, x_ref, g_ref, b_ref, xo_ref, xob_ref):
    sub = OUT_PROJ_SUB_ROWS
    blocks = [slice(r, r + sub) for r in range(0, x_ref.shape[0], sub)]

    def project(rows):
        y = None
        for n, m_ref in enumerate((m0_ref, m1_ref, m2_ref, m3_ref)):
            part = _dot(m_ref[rows, :], w_ref[n * GROUP_WIDTH:(n + 1) * GROUP_WIDTH, :])
            y = part if y is None else y + part
        return y

    def finish(rows, y):
        out = _layer_norm(DEEPNORM_ALPHA * x_ref[rows, :] + y, g_ref[...], b_ref[...])
        xo_ref[rows, :] = out
        xob_ref[rows, :] = out.astype(_BF16)

    _pipelined(blocks, project, finish)


OUT_PROJ_SUB_ROWS = 128


def _out_projection(mixed, w, x, g, b, tm=512):
    m, d = x.shape
    row = lambda i: (i, 0)
    const = lambda i: (0, 0)
    return pl.pallas_call(
        _out_proj_kernel,
        grid=(m // tm,),
        in_specs=[pl.BlockSpec((tm, GROUP_WIDTH), row)] * 4 + [
            pl.BlockSpec((d, d), const), pl.BlockSpec((tm, d), row),
            pl.BlockSpec((1, d), const), pl.BlockSpec((1, d), const)],
        out_specs=[pl.BlockSpec((tm, d), row), pl.BlockSpec((tm, d), row)],
        out_shape=[jax.ShapeDtypeStruct((m, d), _F32), jax.ShapeDtypeStruct((m, d), _BF16)],
        compiler_params=_params("arbitrary"),
    )(*mixed, w, x, g, b)


MLP_LN_SUB_ROWS = 128


def _mlp_kernel(xb_ref, x_ref, w1_hbm, w2_hbm, g_ref, b_ref, xo_ref, xob_ref,
                w1_buf, w2_buf, acc_ref, sems, *, tf):
    n_chunks = w1_hbm.shape[1] // tf
    assert n_chunks % 2 == 0 and n_chunks >= 2
    step, n_steps = pl.program_id(0), pl.num_programs(0)

    def w1_copy(c):
        return pltpu.make_async_copy(w1_hbm.at[:, pl.ds(c * tf, tf)], w1_buf.at[c % 2], sems.at[0, c % 2])

    def w2_copy(c):
        return pltpu.make_async_copy(w2_hbm.at[pl.ds(c * tf, tf), :], w2_buf.at[c % 2], sems.at[1, c % 2])

    def start(copy_of, c):
        if c < n_chunks:
            copy_of(c).start()
        else:
            pl.when(step + 1 < n_steps)(lambda: copy_of(c - n_chunks).start())

    @pl.when(step == 0)
    def _():
        w1_copy(0).start()
        w2_copy(0).start()
        w1_copy(1).start()

    w1_copy(0).wait()
    hidden = _dot(xb_ref[...], w1_buf[0])
    for c in range(n_chunks):
        start(w1_copy, c + 2)
        start(w2_copy, c + 1)
        if c + 1 < n_chunks:
            w1_copy(c + 1).wait()
        w2_copy(c).wait()
        a = jnp.maximum(hidden, 0.0)
        a = (a * a).astype(_BF16)
        if c + 1 < n_chunks:
            ahead = _dot(xb_ref[...], w1_buf[(c + 1) % 2])
            part = _dot(a, w2_buf[c % 2])
            if c == 0:
                acc_ref[...] = part
            else:
                acc_ref[...] += part
            hidden = ahead

    def last_part(rows):
        return _dot(a[rows, :], w2_buf[(n_chunks - 1) % 2])

    def finish(rows, part):
        y = acc_ref[rows, :] + part
        out = _layer_norm(DEEPNORM_ALPHA * x_ref[rows, :] + y, g_ref[...], b_ref[...])
        xo_ref[rows, :] = out
        xob_ref[rows, :] = out.astype(_BF16)

    _pipelined([slice(r, r + MLP_LN_SUB_ROWS) for r in range(0, x_ref.shape[0], MLP_LN_SUB_ROWS)],
               last_part, finish)


def _mlp(xb, x, w1, w2, g, b, tm=512, tf=1024):
    m, d = x.shape
    row = lambda i: (i, 0)
    const = lambda i: (0, 0)
    hbm = pl.BlockSpec(memory_space=pl.ANY)
    return pl.pallas_call(
        functools.partial(_mlp_kernel, tf=tf),
        grid=(m // tm,),
        in_specs=[pl.BlockSpec((tm, d), row), pl.BlockSpec((tm, d), row), hbm, hbm,
                  pl.BlockSpec((1, d), const), pl.BlockSpec((1, d), const)],
        out_specs=[pl.BlockSpec((tm, d), row), pl.BlockSpec((tm, d), row)],
        out_shape=[jax.ShapeDtypeStruct((m, d), _F32), jax.ShapeDtypeStruct((m, d), _BF16)],
        scratch_shapes=[pltpu.VMEM((2, d, tf), _BF16), pltpu.VMEM((2, tf, d), _BF16),
                        pltpu.VMEM((tm, d), _F32), pltpu.SemaphoreType.DMA((2, 2))],
        compiler_params=_params("arbitrary"),
    )(xb, x, w1, w2, g, b)


def _alibi_slopes():
    n = N_DIFF_HEADS + N_DIL_HEADS
    return jnp.asarray(np.exp2(-ALIBI_MAX_EXP * np.arange(1, n + 1) / n), _F32)


def _layer(x, xb, layer_idx, wt_in, b_fz, lam_vecs, subln_g, w_out, ln1_g, ln1_b,
           ln2_g, ln2_b, slopes, dil_tables, batch, casts_todo, mlp_weights):
    m, d = x.shape
    s = m // batch
    h3 = _in_projection(xb, wt_in, layer_idx).reshape(batch, s, -1)

    ct, w_out_b = _forget_cumsum(xb, batch, wt_in, b_fz, w_out, layer_idx)
    ct = ct.reshape(batch, N_FOX_HEADS, 1, s)

    def attention(kernel, *args, **kwargs):
        if not casts_todo:
            return _attention_call(kernel, h3, *args, **kwargs)
        o, w_b = _attention_call(kernel, h3, *args, cast=casts_todo.pop(0), **kwargs)
        mlp_weights.append(w_b)
        return o

    o_fox = attention(
        _fox_kernel,
        [pl.BlockSpec((1, HEADS_PER_BLOCK, 1, s), lambda b, g: (b, g, 0, 0))],
        (ct,), (_SEG["fq"], _SEG["fk"], _SEG["fv"]))

    o_sb = attention(_sb_kernel, [], (), (_SEG["sq"], _SEG["sk"], _SEG["sv"]))

    lam_init = 0.8 - 0.6 * math.exp(-0.3 * layer_idx)
    o_diff = attention(
        functools.partial(_diff_kernel, lam_init=lam_init),
        [pl.BlockSpec((4, HEAD_DIM), lambda b, g, *_: (0, 0)),
         pl.BlockSpec((1, 2 * HEAD_DIM), lambda b, g, *_: (0, 0))],
        (lam_vecs, subln_g), (_SEG["dq"], _SEG["dk"], _SEG["dv"]), scalars=(slopes,))

    o_dil = attention(
        _dil_kernel,
        [pl.BlockSpec((ATT_T, s), lambda b, g, *_: (0, 0))] * 2,
        dil_tables, (_SEG["gq"], _SEG["gk"], _SEG["gv"]), scalars=(slopes,),
        scratch_shapes=[pltpu.VMEM((HEADS_PER_BLOCK, ATT_T, s), _F32)])

    mixed = [o.reshape(m, GROUP_WIDTH) for o in (o_fox, o_sb, o_diff, o_dil)]
    x, xb = _out_projection(mixed, w_out_b, x, ln1_g, ln1_b)
    w1, w2 = mlp_weights[2 * layer_idx:2 * layer_idx + 2]
    return _mlp(xb, x, w1, w2, ln2_g, ln2_b)


def kernel(x, w_in, fox_forget_bias, diff_lambda_q1, diff_lambda_k1, diff_lambda_q2, diff_lambda_k2,
           diff_subln_gain, w_out, ln1_gain, ln1_bias, w_mlp_in, w_mlp_out, ln2_gain, ln2_bias):
    batch, s, d = x.shape
    assert d == D_MODEL and s % ATT_T == 0
    slopes = _alibi_slopes()
    dil_tables = _dilated_tables(s)
    xf = x.reshape(batch * s, d)
    xb = xf.astype(_BF16)
    wt_in = jnp.transpose(w_in, (0, 2, 1))
    casts_todo = [(w, l) for l in range(DEPTH) for w in (w_mlp_in, w_mlp_out)]
    mlp_weights = []
    for l in range(DEPTH):
        b_fz = fox_forget_bias[l].reshape(N_FOX_HEADS, 1)
        lam_vecs = jnp.stack([diff_lambda_q1[l], diff_lambda_k1[l], diff_lambda_q2[l], diff_lambda_k2[l]])
        xf, xb = _layer(
            xf, xb, l, wt_in, b_fz, lam_vecs, diff_subln_gain[l].reshape(1, -1),
            w_out, ln1_gain[l].reshape(1, d), ln1_bias[l].reshape(1, d),
            ln2_gain[l].reshape(1, d), ln2_bias[l].reshape(1, d), slopes, dil_tables, batch,
            casts_todo, mlp_weights)
    return xf.reshape(batch, s, d)
```

```python
import functools
import math

import numpy as np
import jax
import jax.numpy as jnp
from jax import lax
from jax.experimental import pallas as pl
from jax.experimental.pallas import tpu as pltpu

D_MODEL = 2048
DEPTH = 2
HEAD_DIM = 64
GROUP_WIDTH = 512
N_FOX_HEADS = 8
N_DIFF_HEADS = 4
N_DIL_HEADS = 8
DILATED_BRANCHES = ((128, 1), (512, 4), (2048, 16))
ALIBI_MAX_EXP = 8.0
LN_EPS = 1e-5
RMS_EPS = 1e-5
DEEPNORM_ALPHA = (2 * DEPTH) ** 0.25
QK_SCALE = HEAD_DIM ** -0.5

LANES = 128
HEADS_PER_BLOCK = LANES // HEAD_DIM
VMEM_LIMIT_BYTES = 56 * 1024 * 1024
MASKED = -1e30

_SEG = {name: (GROUP_WIDTH // LANES) * i for i, name in enumerate(
    ("fq", "fk", "fv", "sq", "sk", "sv", "dq", "dk", "dv", "gq", "gk", "gv"))}

ATT_T = 256

_F32 = jnp.float32
_BF16 = jnp.bfloat16


def _params(*sem):
    return pltpu.CompilerParams(dimension_semantics=sem, vmem_limit_bytes=VMEM_LIMIT_BYTES)


def _dot(a, b):
    return jnp.dot(a, b, preferred_element_type=_F32)


def _dot_nt(a, b):
    return lax.dot_general(a, b, (((1,), (1,)), ((), ())), preferred_element_type=_F32)


FORGET_ROW0 = 3 * GROUP_WIDTH


def _in_proj_kernel(x_ref, wt_ref, o_ref, wb_ref):
    @pl.when(pl.program_id(1) == 0)
    def _():
        wb_ref[...] = wt_ref[0].astype(_BF16)

    o_ref[...] = _dot_nt(x_ref[...], wb_ref[...]).astype(o_ref.dtype)


def _in_projection(xb, wt, layer, tm=2048, tn=768):
    m, k = xb.shape
    n = wt.shape[1] - N_FOX_HEADS
    assert FORGET_ROW0 % tn == 0

    def w_rows(j, i):
        groups = j * (tn // N_FOX_HEADS) + jnp.where(j * tn >= FORGET_ROW0, 1, 0)
        return (layer, groups * N_FOX_HEADS, 0)

    return pl.pallas_call(
        _in_proj_kernel,
        grid=(n // tn, m // tm),
        in_specs=[pl.BlockSpec((tm, k), lambda j, i: (i, 0)),
                  pl.BlockSpec((pl.Element(1), pl.Element(tn), pl.Element(k)), w_rows)],
        out_specs=pl.BlockSpec((tm, tn), lambda j, i: (i, j)),
        out_shape=jax.ShapeDtypeStruct((m, n), _BF16),
        scratch_shapes=[pltpu.VMEM((tn, k), _BF16)],
        compiler_params=_params("arbitrary", "arbitrary"),
    )(xb, wt)


def _split3(x):
    hi = x.astype(_BF16)
    r = x - hi.astype(_F32)
    mid = r.astype(_BF16)
    lo = (r - mid.astype(_F32)).astype(_BF16)
    return hi, mid, lo


def _log_sigmoid(z):
    return jnp.minimum(z, 0.0) - jnp.log(1.0 + jnp.exp(-jnp.abs(z)))


def _forget_kernel(x_ref, w_ref, b_ref, wo_ref, ct_ref, wob_ref, *rest, blk):
    *maybe_xb_ref, carry_ref = rest
    wob_ref[...] = wo_ref[...].astype(wob_ref.dtype)
    xb = x_ref[...].astype(_BF16)
    for xb_ref in maybe_xb_ref:
        xb_ref[...] = xb

    @pl.when(pl.program_id(1) == 0)
    def _():
        carry_ref[...] = jnp.zeros_like(carry_ref)

    z = _dot_nt(w_ref[...].astype(_BF16), xb) + b_ref[...]
    log_f = _log_sigmoid(z)
    r = lax.broadcasted_iota(jnp.int32, (blk, blk), 0)
    c = lax.broadcasted_iota(jnp.int32, (blk, blk), 1)
    upto = (r <= c).astype(_BF16)
    carry = carry_ref[...]
    for i in range(x_ref.shape[0] // blk):
        hi, mid, lo = _split3(log_f[:, i * blk:(i + 1) * blk])
        cum = _dot(hi, upto) + _dot(mid, upto) + _dot(lo, upto) + carry
        ct_ref[0, :, i * blk:(i + 1) * blk] = cum
        carry = cum[:, blk - 1:blk]
    carry_ref[...] = carry


def _forget_cumsum(x, b, wt, b_fz, w_out, layer, pieces=2, blk=256):
    m, d = x.shape
    rows = m // (b * pieces)
    slab = w_out.shape[1] // (b * pieces)
    flat = lambda i, j: (i * pieces + j, 0)
    out_specs = [pl.BlockSpec((1, N_FOX_HEADS, rows), lambda i, j: (i, 0, j)),
                 pl.BlockSpec((slab, w_out.shape[2]), flat)]
    out_shape = [jax.ShapeDtypeStruct((b, N_FOX_HEADS, m // b), _F32),
                 jax.ShapeDtypeStruct(w_out.shape[1:], _BF16)]
    if x.dtype != _BF16:
        out_specs.append(pl.BlockSpec((rows, d), flat))
        out_shape.append(jax.ShapeDtypeStruct((m, d), _BF16))
    return pl.pallas_call(
        functools.partial(_forget_kernel, blk=blk),
        grid=(b, pieces),
        in_specs=[pl.BlockSpec((rows, d), flat),
                  pl.BlockSpec((None, N_FOX_HEADS, d), lambda i, j: (layer, FORGET_ROW0 // N_FOX_HEADS, 0)),
                  pl.BlockSpec((N_FOX_HEADS, 1), lambda i, j: (0, 0)),
                  pl.BlockSpec((None, slab, w_out.shape[2]), lambda i, j: (layer, i * pieces + j, 0))],
        out_specs=out_specs,
        out_shape=out_shape,
        scratch_shapes=[pltpu.VMEM((N_FOX_HEADS, 1), _F32)],
        compiler_params=_params("arbitrary", "arbitrary"),
    )(x, wt, b_fz, w_out)


def _triangle(t):
    r = lax.broadcasted_iota(jnp.int32, (t, t), 0)
    c = lax.broadcasted_iota(jnp.int32, (t, t), 1)
    return r, c


def _head_lanes(h):
    lane = lax.broadcasted_iota(jnp.int32, (1, LANES), 1)
    return (lane >= h * HEAD_DIM) & (lane < (h + 1) * HEAD_DIM)


def _one_head(q_pair, h):
    return jnp.where(_head_lanes(h), q_pair, jnp.zeros_like(q_pair))


def _masked_with_max(s, causal):
    t, n = s.shape
    if causal is not None:
        diag = jnp.where(causal, s[:, n - t:], MASKED)
        s = diag if n == t else jnp.concatenate([s[:, :n - t], diag], axis=-1)
    return s, jnp.max(s, axis=-1, keepdims=True)


def _values_with_ones(v_pair, h):
    return jnp.where(_head_lanes(h), v_pair, jnp.ones_like(v_pair))


def _normalise(o):
    return o / pltpu.roll(o, HEAD_DIM, 1)


def _pipelined(units, *stages):
    carried = [{} for _ in stages]
    for step in range(len(units) + len(stages) - 1):
        for k, stage in enumerate(stages):
            u = step - k
            if 0 <= u < len(units):
                args = (units[u], carried[k - 1].pop(u)) if k else (units[u],)
                carried[k][u] = stage(*args)


def _fox_kernel(q_ref, k_ref, v_ref, ct_ref, o_ref):
    t = ATT_T
    s_len = q_ref.shape[1]
    row, col = _triangle(t)
    causal = row >= col
    first = _head_lanes(0)
    values = [_values_with_ones(v_ref[0], h) for h in range(HEADS_PER_BLOCK)]

    def scores(unit):
        i, h = unit
        n = (i + 1) * t
        c_row = ct_ref[0, h, :, :n]
        key_bias = c_row[:, n - 1:n] - c_row
        q_pair = q_ref[0, i * t:n, :] * QK_SCALE
        return _masked_with_max(_dot_nt(_one_head(q_pair, h), k_ref[0, :n, :]) + key_bias, causal)

    outs = {}

    def finish(unit, logits):
        i, h = unit
        n = (i + 1) * t
        s, m = logits
        outs[h] = _normalise(_dot(jnp.exp(s - m).astype(_BF16), values[h][:n]))
        if h == HEADS_PER_BLOCK - 1:
            o_ref[0, i * t:n, :] = jnp.where(first, outs[0], outs[1]).astype(o_ref.dtype)

    _pipelined([(i, h) for i in range(s_len // t) for h in range(HEADS_PER_BLOCK)], scores, finish)


def _sb_kernel(q_ref, k_ref, v_ref, o_ref):
    t = ATT_T
    s_len = q_ref.shape[1]
    row, col = _triangle(t)
    strict = row > col
    from_ones = (row >= col).astype(_BF16)
    from_ones2 = jnp.concatenate([from_ones, from_ones], axis=0)
    first = _head_lanes(0)

    def scores(unit):
        i, h = unit
        q_pair = q_ref[0, i * t:(i + 1) * t, :] * QK_SCALE
        return _dot_nt(_one_head(q_pair, h), k_ref[0, :(i + 1) * t, :])

    outs = {}

    def finish(unit, z):
        i, h = unit
        n = (i + 1) * t
        nz = -z
        log_rest = jnp.minimum(nz, 0.0) - jnp.log(1.0 + jnp.exp(jnp.minimum(z, nz)))
        incl, sums = [], []
        for c in range(i + 1):
            lr = log_rest[:, c * t:(c + 1) * t]
            if c == i:
                lr = jnp.where(strict, lr, 0.0)
            hi = lr.astype(_BF16)
            lo = (lr - hi.astype(_F32)).astype(_BF16)
            incl.append(_dot(jnp.concatenate([hi, lo], axis=-1), from_ones2))
            sums.append(incl[c][:, :1])
        tail = jnp.zeros((t, 1), _F32)
        chunks = [None] * (i + 1)
        for c in range(i, -1, -1):
            a = jnp.exp(z[:, c * t:(c + 1) * t] + incl[c] + tail)
            if c == i:
                a = jnp.where(strict, a, 0.0)
            chunks[c] = a.astype(_BF16)
            tail = tail + sums[c]
        a_all = chunks[0] if i == 0 else jnp.concatenate(chunks, axis=-1)
        outs[h] = _dot(a_all, v_ref[0, :n, :])
        if h == HEADS_PER_BLOCK - 1:
            o_ref[0, i * t:n, :] = jnp.where(first, outs[0], outs[1]).astype(o_ref.dtype)

    _pipelined([(i, h) for i in range(s_len // t) for h in range(HEADS_PER_BLOCK)], scores, finish)


def _diff_kernel(slopes_ref, q_ref, k_ref, v_ref, lam_ref, g_ref, o_ref, *, lam_init):
    t = ATT_T
    s_len = q_ref.shape[1]
    slope = slopes_ref[pl.program_id(1)]
    row, col = _triangle(t)
    causal = row >= col
    lq1, lk1, lq2, lk2 = (lam_ref[n:n + 1, :] for n in range(4))
    lam = (jnp.exp(jnp.sum(lq1 * lk1, axis=-1, keepdims=True))
           - jnp.exp(jnp.sum(lq2 * lk2, axis=-1, keepdims=True)) + lam_init)
    gain = g_ref[...] * (1.0 - lam_init)

    def scores(unit):
        i, h = unit
        n = (i + 1) * t
        q_pair = q_ref[0, i * t:n, :] * QK_SCALE
        key_bias = slope * (lax.broadcasted_iota(jnp.int32, (1, n), 1) - (n - 1)).astype(_F32)
        return _masked_with_max(_dot_nt(_one_head(q_pair, h), k_ref[0, :n, :]) + key_bias, causal)

    values = jnp.concatenate([v_ref[0], jnp.ones_like(v_ref[0])], axis=-1)
    maps = {}

    def finish(unit, logits):
        i, h = unit
        n = (i + 1) * t
        s, m = logits
        o = _dot(jnp.exp(s - m).astype(_BF16), values[:n])
        maps[h] = o[:, :LANES] / o[:, LANES:]
        if h == 1:
            od = maps[0] - lam * maps[1]
            od = od * lax.rsqrt(jnp.mean(od * od, axis=-1, keepdims=True) + RMS_EPS)
            o_ref[0, i * t:n, :] = (od * gain).astype(o_ref.dtype)

    _pipelined([(i, h) for i in range(s_len // t) for h in range(2)], scores, finish)


def _dil_kernel(slopes_ref, q_ref, k_ref, v_ref, logm_ref, ndist_ref, o_ref, bias_ref):
    t = ATT_T
    s_len = q_ref.shape[1]
    first = _head_lanes(0)
    values = [_values_with_ones(v_ref[0], h) for h in range(HEADS_PER_BLOCK)]
    for h in range(HEADS_PER_BLOCK):
        slope = slopes_ref[N_DIFF_HEADS + HEADS_PER_BLOCK * pl.program_id(1) + h]
        bias_ref[h] = logm_ref[...] + slope * ndist_ref[...]

    def scores(unit):
        i, h = unit
        n = (i + 1) * t
        q_pair = q_ref[0, i * t:n, :] * QK_SCALE
        s = _dot_nt(_one_head(q_pair, h), k_ref[0, :n, :]) + bias_ref[h, :, s_len - n:]
        return _masked_with_max(s, None)

    outs = {}

    def finish(unit, logits):
        i, h = unit
        n = (i + 1) * t
        s, m = logits
        outs[h] = _normalise(_dot(jnp.exp(s - m).astype(_BF16), values[h][:n]))
        if h == HEADS_PER_BLOCK - 1:
            o_ref[0, i * t:n, :] = jnp.where(first, outs[0], outs[1]).astype(o_ref.dtype)

    _pipelined([(i, h) for i in range(s_len // t) for h in range(HEADS_PER_BLOCK)], scores, finish)


def _also_casting(kernel, n_lead):
    def body(*refs):
        src, o_ref, dst = refs[n_lead:n_lead + 3]
        dst[...] = src[...].astype(dst.dtype)
        kernel(*refs[:n_lead], o_ref, *refs[n_lead + 3:])
    return body


def _attention_call(kernel, h3, extra_specs, extra_args, segs, scalars=(), scratch_shapes=(), cast=None):
    b, s, _ = h3.shape
    seg_q, seg_k, seg_v = segs
    n_groups = GROUP_WIDTH // LANES

    def col_block(seg):
        return pl.BlockSpec((1, s, LANES), lambda bb, g, *_: (bb, 0, seg + g))

    in_specs = [col_block(seg_q), col_block(seg_k), col_block(seg_v)] + list(extra_specs)
    out_specs = col_block(0)
    out_shape = jax.ShapeDtypeStruct((b, s, GROUP_WIDTH), _BF16)
    args = (*scalars, h3, h3, h3, *extra_args)
    if cast is not None:
        w, layer = cast
        _, rows, cols = w.shape
        slab = rows // (b * n_groups)
        kernel = _also_casting(kernel, len(args))
        in_specs.append(pl.BlockSpec((None, slab, cols), lambda bb, g, *_: (layer, bb * n_groups + g, 0)))
        out_specs = [out_specs, pl.BlockSpec((slab, cols), lambda bb, g, *_: (bb * n_groups + g, 0))]
        out_shape = [out_shape, jax.ShapeDtypeStruct((rows, cols), _BF16)]
        args = (*args, w)
    grid_spec = pltpu.PrefetchScalarGridSpec(
        num_scalar_prefetch=len(scalars),
        grid=(b, n_groups),
        in_specs=in_specs,
        out_specs=out_specs,
        scratch_shapes=list(scratch_shapes))
    return pl.pallas_call(
        kernel,
        grid_spec=grid_spec,
        out_shape=out_shape,
        compiler_params=_params("arbitrary", "arbitrary"),
    )(*args)


def _dilated_tables(s):
    r = lax.broadcasted_iota(jnp.int32, (ATT_T, s), 0)
    c = lax.broadcasted_iota(jnp.int32, (ATT_T, s), 1)
    dist = (s - ATT_T) + r - c
    mult = jnp.zeros(dist.shape, _F32)
    for window, dilation in DILATED_BRANCHES:
        hit = (dist >= 0) & (dist <= window) & (dist % dilation == 0)
        mult = mult + hit.astype(_F32)
    logm = jnp.where(mult > 0, jnp.log(jnp.maximum(mult, 1.0)), MASKED)
    return logm, -jnp.maximum(dist, 0).astype(_F32)


def _layer_norm(z, g, b):
    mu = jnp.mean(z, axis=-1, keepdims=True)
    zc = z - mu
    var = jnp.mean(zc * zc, axis=-1, keepdims=True)
    return zc * lax.rsqrt(var + LN_EPS) * g + b


def _out_proj_kernel(m0_ref, m1_ref, m2_ref, m3_ref, w_ref, x_ref, g_ref, b_ref, xo_ref, xob_ref):
    sub = OUT_PROJ_SUB_ROWS
    blocks = [slice(r, r + sub) for r in range(0, x_ref.shape[0], sub)]

    def project(rows):
        mixed = jnp.concatenate([m_ref[rows, :] for m_ref in (m0_ref, m1_ref, m2_ref, m3_ref)], axis=-1)
        return _dot(mixed, w_ref[...])

    def finish(rows, y):
        out = _layer_norm(DEEPNORM_ALPHA * x_ref[rows, :] + y, g_ref[...], b_ref[...])
        xo_ref[rows, :] = out
        xob_ref[rows, :] = out.astype(_BF16)

    _pipelined(blocks, project, finish)


OUT_PROJ_SUB_ROWS = 128


def _out_projection(mixed, w, x, g, b, tm=512):
    m, d = x.shape
    row = lambda i: (i, 0)
    const = lambda i: (0, 0)
    return pl.pallas_call(
        _out_proj_kernel,
        grid=(m // tm,),
        in_specs=[pl.BlockSpec((tm, GROUP_WIDTH), row)] * 4 + [
            pl.BlockSpec((d, d), const), pl.BlockSpec((tm, d), row),
            pl.BlockSpec((1, d), const), pl.BlockSpec((1, d), const)],
        out_specs=[pl.BlockSpec((tm, d), row), pl.BlockSpec((tm, d), row)],
        out_shape=[jax.ShapeDtypeStruct((m, d), _F32), jax.ShapeDtypeStruct((m, d), _BF16)],
        compiler_params=_params("arbitrary"),
    )(*mixed, w, x, g, b)


MLP_LN_SUB_ROWS = 128


def _mlp_kernel(xb_ref, x_ref, w1_hbm, w2_hbm, g_ref, b_ref, xo_ref, xob_ref,
                w1_buf, w2_buf, acc_ref, sems, *, tf):
    n_chunks = w1_hbm.shape[1] // tf
    assert n_chunks % 2 == 0 and n_chunks >= 2
    step, n_steps = pl.program_id(0), pl.num_programs(0)

    def w1_copy(c):
        return pltpu.make_async_copy(w1_hbm.at[:, pl.ds(c * tf, tf)], w1_buf.at[c % 2], sems.at[0, c % 2])

    def w2_copy(c):
        return pltpu.make_async_copy(w2_hbm.at[pl.ds(c * tf, tf), :], w2_buf.at[c % 2], sems.at[1, c % 2])

    def start(copy_of, c):
        if c < n_chunks:
            copy_of(c).start()
        else:
            pl.when(step + 1 < n_steps)(lambda: copy_of(c - n_chunks).start())

    @pl.when(step == 0)
    def _():
        w1_copy(0).start()
        w2_copy(0).start()
        w1_copy(1).start()

    w1_copy(0).wait()
    hidden = _dot(xb_ref[...], w1_buf[0])
    for c in range(n_chunks):
        start(w1_copy, c + 2)
        start(w2_copy, c + 1)
        if c + 1 < n_chunks:
            w1_copy(c + 1).wait()
        w2_copy(c).wait()
        a = jnp.maximum(hidden, 0.0)
        a = (a * a).astype(_BF16)
        if c + 1 < n_chunks:
            ahead = _dot(xb_ref[...], w1_buf[(c + 1) % 2])
            part = _dot(a, w2_buf[c % 2])
            if c == 0:
                acc_ref[...] = part
            else:
                acc_ref[...] += part
            hidden = ahead

    def last_part(rows):
        return _dot(a[rows, :], w2_buf[(n_chunks - 1) % 2])

    def finish(rows, part):
        y = acc_ref[rows, :] + part
        out = _layer_norm(DEEPNORM_ALPHA * x_ref[rows, :] + y, g_ref[...], b_ref[...])
        xo_ref[rows, :] = out
        xob_ref[rows, :] = out.astype(_BF16)

    _pipelined([slice(r, r + MLP_LN_SUB_ROWS) for r in range(0, x_ref.shape[0], MLP_LN_SUB_ROWS)],
               last_part, finish)


def _mlp(xb, x, w1, w2, g, b, tm=512, tf=1024):
    m, d = x.shape
    row = lambda i: (i, 0)
    const = lambda i: (0, 0)
    hbm = pl.BlockSpec(memory_space=pl.ANY)
    return pl.pallas_call(
        functools.partial(_mlp_kernel, tf=tf),
        grid=(m // tm,),
        in_specs=[pl.BlockSpec((tm, d), row), pl.BlockSpec((tm, d), row), hbm, hbm,
                  pl.BlockSpec((1, d), const), pl.BlockSpec((1, d), const)],
        out_specs=[pl.BlockSpec((tm, d), row), pl.BlockSpec((tm, d), row)],
        out_shape=[jax.ShapeDtypeStruct((m, d), _F32), jax.ShapeDtypeStruct((m, d), _BF16)],
        scratch_shapes=[pltpu.VMEM((2, d, tf), _BF16), pltpu.VMEM((2, tf, d), _BF16),
                        pltpu.VMEM((tm, d), _F32), pltpu.SemaphoreType.DMA((2, 2))],
        compiler_params=_params("arbitrary"),
    )(xb, x, w1, w2, g, b)


def _alibi_slopes():
    n = N_DIFF_HEADS + N_DIL_HEADS
    return jnp.asarray(np.exp2(-ALIBI_MAX_EXP * np.arange(1, n + 1) / n), _F32)


def _layer(x, xb, layer_idx, wt_in, b_fz, lam_vecs, subln_g, w_out, ln1_g, ln1_b,
           ln2_g, ln2_b, slopes, dil_tables, batch, casts_todo, mlp_weights):
    m, d = x.shape
    s = m // batch
    if xb is None:
        ct, w_out_b, xb = _forget_cumsum(x, batch, wt_in, b_fz, w_out, layer_idx)
    else:
        ct, w_out_b = _forget_cumsum(xb, batch, wt_in, b_fz, w_out, layer_idx)
    ct = ct.reshape(batch, N_FOX_HEADS, 1, s)
    h3 = _in_projection(xb, wt_in, layer_idx).reshape(batch, s, -1)

    def attention(kernel, *args, **kwargs):
        if not casts_todo:
            return _attention_call(kernel, h3, *args, **kwargs)
        o, w_b = _attention_call(kernel, h3, *args, cast=casts_todo.pop(0), **kwargs)
        mlp_weights.append(w_b)
        return o

    o_fox = attention(
        _fox_kernel,
        [pl.BlockSpec((1, HEADS_PER_BLOCK, 1, s), lambda b, g: (b, g, 0, 0))],
        (ct,), (_SEG["fq"], _SEG["fk"], _SEG["fv"]))

    o_sb = attention(_sb_kernel, [], (), (_SEG["sq"], _SEG["sk"], _SEG["sv"]))

    lam_init = 0.8 - 0.6 * math.exp(-0.3 * layer_idx)
    o_diff = attention(
        functools.partial(_diff_kernel, lam_init=lam_init),
        [pl.BlockSpec((4, HEAD_DIM), lambda b, g, *_: (0, 0)),
         pl.BlockSpec((1, 2 * HEAD_DIM), lambda b, g, *_: (0, 0))],
        (lam_vecs, subln_g), (_SEG["dq"], _SEG["dk"], _SEG["dv"]), scalars=(slopes,))

    o_dil = attention(
        _dil_kernel,
        [pl.BlockSpec((ATT_T, s), lambda b, g, *_: (0, 0))] * 2,
        dil_tables, (_SEG["gq"], _SEG["gk"], _SEG["gv"]), scalars=(slopes,),
        scratch_shapes=[pltpu.VMEM((HEADS_PER_BLOCK, ATT_T, s), _F32)])

    mixed = [o.reshape(m, GROUP_WIDTH) for o in (o_fox, o_sb, o_diff, o_dil)]
    x, xb = _out_projection(mixed, w_out_b, x, ln1_g, ln1_b)
    w1, w2 = mlp_weights[2 * layer_idx:2 * layer_idx + 2]
    return _mlp(xb, x, w1, w2, ln2_g, ln2_b)


def kernel(x, w_in, fox_forget_bias, diff_lambda_q1, diff_lambda_k1, diff_lambda_q2, diff_lambda_k2,
           diff_subln_gain, w_out, ln1_gain, ln1_bias, w_mlp_in, w_mlp_out, ln2_gain, ln2_bias):
    batch, s, d = x.shape
    assert d == D_MODEL and s % ATT_T == 0
    slopes = _alibi_slopes()
    dil_tables = _dilated_tables(s)
    xf = x.reshape(batch * s, d)
    xb = None
    wt_in = jnp.transpose(w_in, (0, 2, 1))
    casts_todo = [(w, l) for l in range(DEPTH) for w in (w_mlp_in, w_mlp_out)]
    mlp_weights = []
    for l in range(DEPTH):
        b_fz = fox_forget_bias[l].reshape(N_FOX_HEADS, 1)
        lam_vecs = jnp.stack([diff_lambda_q1[l], diff_lambda_k1[l], diff_lambda_q2[l], diff_lambda_k2[l]])
        xf, xb = _layer(
            xf, xb, l, wt_in, b_fz, lam_vecs, diff_subln_gain[l].reshape(1, -1),
            w_out, ln1_gain[l].reshape(1, d), ln1_bias[l].reshape(1, d),
            ln2_gain[l].reshape(1, d), ln2_bias[l].reshape(1, d), slopes, dil_tables, batch,
            casts_todo, mlp_weights)
    return xf.reshape(batch, s, d)
```

```python
import functools
import math

import numpy as np
import jax
import jax.numpy as jnp
from jax import lax
from jax.experimental import pallas as pl
from jax.experimental.pallas import tpu as pltpu

D_MODEL = 2048
DEPTH = 2
HEAD_DIM = 64
GROUP_WIDTH = 512
N_FOX_HEADS = 8
N_DIFF_HEADS = 4
N_DIL_HEADS = 8
DILATED_BRANCHES = ((128, 1), (512, 4), (2048, 16))
ALIBI_MAX_EXP = 8.0
LN_EPS = 1e-5
RMS_EPS = 1e-5
DEEPNORM_ALPHA = (2 * DEPTH) ** 0.25
QK_SCALE = HEAD_DIM ** -0.5

LANES = 128
HEADS_PER_BLOCK = LANES // HEAD_DIM
VMEM_LIMIT_BYTES = 56 * 1024 * 1024
MASKED = -1e30

_SEG = {name: (GROUP_WIDTH // LANES) * i for i, name in enumerate(
    ("fq", "fk", "fv", "sq", "sk", "sv", "dq", "dk", "dv", "gq", "gk", "gv"))}

ATT_T = 256

_F32 = jnp.float32
_BF16 = jnp.bfloat16


def _params(*sem):
    return pltpu.CompilerParams(dimension_semantics=sem, vmem_limit_bytes=VMEM_LIMIT_BYTES)


def _dot(a, b):
    return jnp.dot(a, b, preferred_element_type=_F32)


def _dot_nt(a, b):
    return lax.dot_general(a, b, (((1,), (1,)), ((), ())), preferred_element_type=_F32)


FORGET_ROW0 = 3 * GROUP_WIDTH


def _in_proj_kernel(x_ref, wt_ref, o_ref, wb_ref):
    @pl.when(pl.program_id(1) == 0)
    def _():
        wb_ref[...] = wt_ref[0].astype(_BF16)

    o_ref[...] = _dot_nt(x_ref[...], wb_ref[...]).astype(o_ref.dtype)


def _in_projection(xb, wt, layer, tm=1024, tn=1536):
    m, k = xb.shape
    n = wt.shape[1] - N_FOX_HEADS
    assert FORGET_ROW0 % tn == 0

    def w_rows(j, i):
        groups = j * (tn // N_FOX_HEADS) + jnp.where(j * tn >= FORGET_ROW0, 1, 0)
        return (layer, groups * N_FOX_HEADS, 0)

    return pl.pallas_call(
        _in_proj_kernel,
        grid=(n // tn, m // tm),
        in_specs=[pl.BlockSpec((tm, k), lambda j, i: (i, 0)),
                  pl.BlockSpec((pl.Element(1), pl.Element(tn), pl.Element(k)), w_rows)],
        out_specs=pl.BlockSpec((tm, tn), lambda j, i: (i, j)),
        out_shape=jax.ShapeDtypeStruct((m, n), _BF16),
        scratch_shapes=[pltpu.VMEM((tn, k), _BF16)],
        compiler_params=_params("arbitrary", "arbitrary"),
    )(xb, wt)


def _split3(x):
    hi = x.astype(_BF16)
    r = x - hi.astype(_F32)
    mid = r.astype(_BF16)
    lo = (r - mid.astype(_F32)).astype(_BF16)
    return hi, mid, lo


def _log_sigmoid(z):
    return jnp.minimum(z, 0.0) - jnp.log(1.0 + jnp.exp(-jnp.abs(z)))


def _forget_kernel(x_ref, w_ref, b_ref, wo_ref, ct_ref, wob_ref, *rest, blk):
    *maybe_xb_ref, carry_ref = rest
    wob_ref[...] = wo_ref[...].astype(wob_ref.dtype)
    xb = x_ref[...].astype(_BF16)
    for xb_ref in maybe_xb_ref:
        xb_ref[...] = xb

    @pl.when(pl.program_id(1) == 0)
    def _():
        carry_ref[...] = jnp.zeros_like(carry_ref)

    z = _dot_nt(w_ref[...].astype(_BF16), xb) + b_ref[...]
    log_f = _log_sigmoid(z)
    r = lax.broadcasted_iota(jnp.int32, (blk, blk), 0)
    c = lax.broadcasted_iota(jnp.int32, (blk, blk), 1)
    upto = (r <= c).astype(_BF16)
    carry = carry_ref[...]
    for i in range(x_ref.shape[0] // blk):
        hi, mid, lo = _split3(log_f[:, i * blk:(i + 1) * blk])
        cum = _dot(hi, upto) + _dot(mid, upto) + _dot(lo, upto) + carry
        ct_ref[0, :, i * blk:(i + 1) * blk] = cum
        carry = cum[:, blk - 1:blk]
    carry_ref[...] = carry


def _forget_cumsum(x, b, wt, b_fz, w_out, layer, pieces=2, blk=256):
    m, d = x.shape
    rows = m // (b * pieces)
    slab = w_out.shape[1] // (b * pieces)
    flat = lambda i, j: (i * pieces + j, 0)
    out_specs = [pl.BlockSpec((1, N_FOX_HEADS, rows), lambda i, j: (i, 0, j)),
                 pl.BlockSpec((slab, w_out.shape[2]), flat)]
    out_shape = [jax.ShapeDtypeStruct((b, N_FOX_HEADS, m // b), _F32),
                 jax.ShapeDtypeStruct(w_out.shape[1:], _BF16)]
    if x.dtype != _BF16:
        out_specs.append(pl.BlockSpec((rows, d), flat))
        out_shape.append(jax.ShapeDtypeStruct((m, d), _BF16))
    return pl.pallas_call(
        functools.partial(_forget_kernel, blk=blk),
        grid=(b, pieces),
        in_specs=[pl.BlockSpec((rows, d), flat),
                  pl.BlockSpec((None, N_FOX_HEADS, d), lambda i, j: (layer, FORGET_ROW0 // N_FOX_HEADS, 0)),
                  pl.BlockSpec((N_FOX_HEADS, 1), lambda i, j: (0, 0)),
                  pl.BlockSpec((None, slab, w_out.shape[2]), lambda i, j: (layer, i * pieces + j, 0))],
        out_specs=out_specs,
        out_shape=out_shape,
        scratch_shapes=[pltpu.VMEM((N_FOX_HEADS, 1), _F32)],
        compiler_params=_params("arbitrary", "arbitrary"),
    )(x, wt, b_fz, w_out)


def _triangle(t):
    r = lax.broadcasted_iota(jnp.int32, (t, t), 0)
    c = lax.broadcasted_iota(jnp.int32, (t, t), 1)
    return r, c


def _head_lanes(h):
    lane = lax.broadcasted_iota(jnp.int32, (1, LANES), 1)
    return (lane >= h * HEAD_DIM) & (lane < (h + 1) * HEAD_DIM)


def _one_head(q_pair, h):
    return jnp.where(_head_lanes(h), q_pair, jnp.zeros_like(q_pair))


def _masked_with_max(s, causal):
    t, n = s.shape
    if causal is not None:
        diag = jnp.where(causal, s[:, n - t:], MASKED)
        s = diag if n == t else jnp.concatenate([s[:, :n - t], diag], axis=-1)
    return s, jnp.max(s, axis=-1, keepdims=True)


def _values_with_ones(v_pair, h):
    return jnp.where(_head_lanes(h), v_pair, jnp.ones_like(v_pair))


def _normalise(o):
    return o / pltpu.roll(o, HEAD_DIM, 1)


def _pipelined(units, *stages):
    carried = [{} for _ in stages]
    for step in range(len(units) + len(stages) - 1):
        for k, stage in enumerate(stages):
            u = step - k
            if 0 <= u < len(units):
                args = (units[u], carried[k - 1].pop(u)) if k else (units[u],)
                carried[k][u] = stage(*args)


def _fox_kernel(q_ref, k_ref, v_ref, ct_ref, o_ref):
    t = ATT_T
    s_len = q_ref.shape[1]
    row, col = _triangle(t)
    causal = row >= col
    first = _head_lanes(0)
    values = [_values_with_ones(v_ref[0], h) for h in range(HEADS_PER_BLOCK)]

    def scores(unit):
        i, h = unit
        n = (i + 1) * t
        c_row = ct_ref[0, h, :, :n]
        key_bias = c_row[:, n - 1:n] - c_row
        q_pair = q_ref[0, i * t:n, :] * QK_SCALE
        return _masked_with_max(_dot_nt(_one_head(q_pair, h), k_ref[0, :n, :]) + key_bias, causal)

    outs = {}

    def finish(unit, logits):
        i, h = unit
        n = (i + 1) * t
        s, m = logits
        outs[h] = _normalise(_dot(jnp.exp(s - m).astype(_BF16), values[h][:n]))
        if h == HEADS_PER_BLOCK - 1:
            o_ref[0, i * t:n, :] = jnp.where(first, outs[0], outs[1]).astype(o_ref.dtype)

    _pipelined([(i, h) for i in range(s_len // t) for h in range(HEADS_PER_BLOCK)], scores, finish)


def _sb_kernel(q_ref, k_ref, v_ref, o_ref):
    t = ATT_T
    s_len = q_ref.shape[1]
    row, col = _triangle(t)
    strict = row > col
    from_ones = (row >= col).astype(_BF16)
    from_ones2 = jnp.concatenate([from_ones, from_ones], axis=0)
    first = _head_lanes(0)

    def scores(unit):
        i, h = unit
        q_pair = q_ref[0, i * t:(i + 1) * t, :] * QK_SCALE
        return _dot_nt(_one_head(q_pair, h), k_ref[0, :(i + 1) * t, :])

    outs = {}

    def cumulate(unit, z):
        i, h = unit
        nz = -z
        log_rest = jnp.minimum(nz, 0.0) - jnp.log(1.0 + jnp.exp(jnp.minimum(z, nz)))
        incl, sums = [], []
        for c in range(i + 1):
            lr = log_rest[:, c * t:(c + 1) * t]
            if c == i:
                lr = jnp.where(strict, lr, 0.0)
            hi = lr.astype(_BF16)
            lo = (lr - hi.astype(_F32)).astype(_BF16)
            incl.append(_dot(jnp.concatenate([hi, lo], axis=-1), from_ones2))
            sums.append(incl[c][:, :1])
        return z, incl, sums

    def finish(unit, carried):
        i, h = unit
        n = (i + 1) * t
        z, incl, sums = carried
        tail = jnp.zeros((t, 1), _F32)
        chunks = [None] * (i + 1)
        for c in range(i, -1, -1):
            a = jnp.exp(z[:, c * t:(c + 1) * t] + incl[c] + tail)
            if c == i:
                a = jnp.where(strict, a, 0.0)
            chunks[c] = a.astype(_BF16)
            tail = tail + sums[c]
        a_all = chunks[0] if i == 0 else jnp.concatenate(chunks, axis=-1)
        outs[h] = _dot(a_all, v_ref[0, :n, :])
        if h == HEADS_PER_BLOCK - 1:
            o_ref[0, i * t:n, :] = jnp.where(first, outs[0], outs[1]).astype(o_ref.dtype)

    _pipelined([(i, h) for i in range(s_len // t) for h in range(HEADS_PER_BLOCK)],
               scores, cumulate, finish)


def _diff_kernel(slopes_ref, q_ref, k_ref, v_ref, lam_ref, g_ref, o_ref, *, lam_init):
    t = ATT_T
    s_len = q_ref.shape[1]
    slope = slopes_ref[pl.program_id(1)]
    row, col = _triangle(t)
    causal = row >= col
    lq1, lk1, lq2, lk2 = (lam_ref[n:n + 1, :] for n in range(4))
    lam = (jnp.exp(jnp.sum(lq1 * lk1, axis=-1, keepdims=True))
           - jnp.exp(jnp.sum(lq2 * lk2, axis=-1, keepdims=True)) + lam_init)
    gain = g_ref[...] * (1.0 - lam_init)

    def scores(unit):
        i, h = unit
        n = (i + 1) * t
        q_pair = q_ref[0, i * t:n, :] * QK_SCALE
        key_bias = slope * (lax.broadcasted_iota(jnp.int32, (1, n), 1) - (n - 1)).astype(_F32)
        return _masked_with_max(_dot_nt(_one_head(q_pair, h), k_ref[0, :n, :]) + key_bias, causal)

    values = jnp.concatenate([v_ref[0], jnp.ones_like(v_ref[0])], axis=-1)
    maps = {}

    def finish(unit, logits):
        i, h = unit
        n = (i + 1) * t
        s, m = logits
        o = _dot(jnp.exp(s - m).astype(_BF16), values[:n])
        maps[h] = o[:, :LANES] / o[:, LANES:]
        if h == 1:
            od = maps[0] - lam * maps[1]
            od = od * lax.rsqrt(jnp.mean(od * od, axis=-1, keepdims=True) + RMS_EPS)
            o_ref[0, i * t:n, :] = (od * gain).astype(o_ref.dtype)

    _pipelined([(i, h) for i in range(s_len // t) for h in range(2)], scores, finish)


def _dil_kernel(slopes_ref, q_ref, k_ref, v_ref, logm_ref, ndist_ref, o_ref, bias_ref):
    t = ATT_T
    s_len = q_ref.shape[1]
    first = _head_lanes(0)
    values = [_values_with_ones(v_ref[0], h) for h in range(HEADS_PER_BLOCK)]
    for h in range(HEADS_PER_BLOCK):
        slope = slopes_ref[N_DIFF_HEADS + HEADS_PER_BLOCK * pl.program_id(1) + h]
        bias_ref[h] = logm_ref[...] + slope * ndist_ref[...]

    def scores(unit):
        i, h = unit
        n = (i + 1) * t
        q_pair = q_ref[0, i * t:n, :] * QK_SCALE
        s = _dot_nt(_one_head(q_pair, h), k_ref[0, :n, :]) + bias_ref[h, :, s_len - n:]
        return _masked_with_max(s, None)

    outs = {}

    def finish(unit, logits):
        i, h = unit
        n = (i + 1) * t
        s, m = logits
        outs[h] = _normalise(_dot(jnp.exp(s - m).astype(_BF16), values[h][:n]))
        if h == HEADS_PER_BLOCK - 1:
            o_ref[0, i * t:n, :] = jnp.where(first, outs[0], outs[1]).astype(o_ref.dtype)

    _pipelined([(i, h) for i in range(s_len // t) for h in range(HEADS_PER_BLOCK)], scores, finish)


def _also_casting(kernel, n_lead):
    def body(*refs):
        src, o_ref, dst = refs[n_lead:n_lead + 3]
        dst[...] = src[...].astype(dst.dtype)
        kernel(*refs[:n_lead], o_ref, *refs[n_lead + 3:])
    return body


def _attention_call(kernel, h3, extra_specs, extra_args, segs, scalars=(), scratch_shapes=(), cast=None):
    b, s, _ = h3.shape
    seg_q, seg_k, seg_v = segs
    n_groups = GROUP_WIDTH // LANES

    def col_block(seg):
        return pl.BlockSpec((1, s, LANES), lambda bb, g, *_: (bb, 0, seg + g))

    in_specs = [col_block(seg_q), col_block(seg_k), col_block(seg_v)] + list(extra_specs)
    out_specs = col_block(0)
    out_shape = jax.ShapeDtypeStruct((b, s, GROUP_WIDTH), _BF16)
    args = (*scalars, h3, h3, h3, *extra_args)
    if cast is not None:
        w, layer = cast
        _, rows, cols = w.shape
        slab = rows // (b * n_groups)
        kernel = _also_casting(kernel, len(args))
        in_specs.append(pl.BlockSpec((None, slab, cols), lambda bb, g, *_: (layer, bb * n_groups + g, 0)))
        out_specs = [out_specs, pl.BlockSpec((slab, cols), lambda bb, g, *_: (bb * n_groups + g, 0))]
        out_shape = [out_shape, jax.ShapeDtypeStruct((rows, cols), _BF16)]
        args = (*args, w)
    grid_spec = pltpu.PrefetchScalarGridSpec(
        num_scalar_prefetch=len(scalars),
        grid=(b, n_groups),
        in_specs=in_specs,
        out_specs=out_specs,
        scratch_shapes=list(scratch_shapes))
    return pl.pallas_call(
        kernel,
        grid_spec=grid_spec,
        out_shape=out_shape,
        compiler_params=_params("arbitrary", "arbitrary"),
    )(*args)


def _dilated_tables(s):
    r = lax.broadcasted_iota(jnp.int32, (ATT_T, s), 0)
    c = lax.broadcasted_iota(jnp.int32, (ATT_T, s), 1)
    dist = (s - ATT_T) + r - c
    mult = jnp.zeros(dist.shape, _F32)
    for window, dilation in DILATED_BRANCHES:
        hit = (dist >= 0) & (dist <= window) & (dist % dilation == 0)
        mult = mult + hit.astype(_F32)
    logm = jnp.where(mult > 0, jnp.log(jnp.maximum(mult, 1.0)), MASKED)
    return logm, -jnp.maximum(dist, 0).astype(_F32)


def _layer_norm(z, g, b):
    mu = jnp.mean(z, axis=-1, keepdims=True)
    zc = z - mu
    var = jnp.mean(zc * zc, axis=-1, keepdims=True)
    return zc * lax.rsqrt(var + LN_EPS) * g + b


def _out_proj_kernel(m0_ref, m1_ref, m2_ref, m3_ref, w_ref, x_ref, g_ref, b_ref, xo_ref, xob_ref):
    sub = OUT_PROJ_SUB_ROWS
    blocks = [slice(r, r + sub) for r in range(0, x_ref.shape[0], sub)]

    def project(rows):
        mixed = jnp.concatenate([m_ref[rows, :] for m_ref in (m0_ref, m1_ref, m2_ref, m3_ref)], axis=-1)
        return _dot(mixed, w_ref[...])

    def finish(rows, y):
        out = _layer_norm(DEEPNORM_ALPHA * x_ref[rows, :] + y, g_ref[...], b_ref[...])
        xo_ref[rows, :] = out
        xob_ref[rows, :] = out.astype(_BF16)

    _pipelined(blocks, project, finish)


OUT_PROJ_SUB_ROWS = 128


def _out_projection(mixed, w, x, g, b, tm=512):
    m, d = x.shape
    row = lambda i: (i, 0)
    const = lambda i: (0, 0)
    return pl.pallas_call(
        _out_proj_kernel,
        grid=(m // tm,),
        in_specs=[pl.BlockSpec((tm, GROUP_WIDTH), row)] * 4 + [
            pl.BlockSpec((d, d), const), pl.BlockSpec((tm, d), row),
            pl.BlockSpec((1, d), const), pl.BlockSpec((1, d), const)],
        out_specs=[pl.BlockSpec((tm, d), row), pl.BlockSpec((tm, d), row)],
        out_shape=[jax.ShapeDtypeStruct((m, d), _F32), jax.ShapeDtypeStruct((m, d), _BF16)],
        compiler_params=_params("arbitrary"),
    )(*mixed, w, x, g, b)


MLP_LN_SUB_ROWS = 128


def _mlp_kernel(xb_ref, x_ref, w1_hbm, w2_hbm, g_ref, b_ref, xo_ref, xob_ref,
                w1_buf, w2_buf, acc_ref, sems, *, tf):
    n_chunks = w1_hbm.shape[1] // tf
    assert n_chunks % 2 == 0 and n_chunks >= 2
    step, n_steps = pl.program_id(0), pl.num_programs(0)

    def w1_copy(c):
        return pltpu.make_async_copy(w1_hbm.at[:, pl.ds(c * tf, tf)], w1_buf.at[c % 2], sems.at[0, c % 2])

    def w2_copy(c):
        return pltpu.make_async_copy(w2_hbm.at[pl.ds(c * tf, tf), :], w2_buf.at[c % 2], sems.at[1, c % 2])

    def start(copy_of, c):
        if c < n_chunks:
            copy_of(c).start()
        else:
            pl.when(step + 1 < n_steps)(lambda: copy_of(c - n_chunks).start())

    @pl.when(step == 0)
    def _():
        w1_copy(0).start()
        w2_copy(0).start()
        w1_copy(1).start()

    w1_copy(0).wait()
    hidden = _dot(xb_ref[...], w1_buf[0])
    for c in range(n_chunks):
        start(w1_copy, c + 2)
        start(w2_copy, c + 1)
        if c + 1 < n_chunks:
            w1_copy(c + 1).wait()
        w2_copy(c).wait()
        a = jnp.maximum(hidden, 0.0)
        a = (a * a).astype(_BF16)
        if c + 1 < n_chunks:
            ahead = _dot(xb_ref[...], w1_buf[(c + 1) % 2])
            part = _dot(a, w2_buf[c % 2])
            if c == 0:
                acc_ref[...] = part
            else:
                acc_ref[...] += part
            hidden = ahead

    def last_part(rows):
        return _dot(a[rows, :], w2_buf[(n_chunks - 1) % 2])

    def finish(rows, part):
        y = acc_ref[rows, :] + part
        out = _layer_norm(DEEPNORM_ALPHA * x_ref[rows, :] + y, g_ref[...], b_ref[...])
        xo_ref[rows, :] = out
        xob_ref[rows, :] = out.astype(_BF16)

    _pipelined([slice(r, r + MLP_LN_SUB_ROWS) for r in range(0, x_ref.shape[0], MLP_LN_SUB_ROWS)],
               last_part, finish)


def _mlp(xb, x, w1, w2, g, b, tm=512, tf=1024):
    m, d = x.shape
    row = lambda i: (i, 0)
    const = lambda i: (0, 0)
    hbm = pl.BlockSpec(memory_space=pl.ANY)
    return pl.pallas_call(
        functools.partial(_mlp_kernel, tf=tf),
        grid=(m // tm,),
        in_specs=[pl.BlockSpec((tm, d), row), pl.BlockSpec((tm, d), row), hbm, hbm,
                  pl.BlockSpec((1, d), const), pl.BlockSpec((1, d), const)],
        out_specs=[pl.BlockSpec((tm, d), row), pl.BlockSpec((tm, d), row)],
        out_shape=[jax.ShapeDtypeStruct((m, d), _F32), jax.ShapeDtypeStruct((m, d), _BF16)],
        scratch_shapes=[pltpu.VMEM((2, d, tf), _BF16), pltpu.VMEM((2, tf, d), _BF16),
                        pltpu.VMEM((tm, d), _F32), pltpu.SemaphoreType.DMA((2, 2))],
        compiler_params=_params("arbitrary"),
    )(xb, x, w1, w2, g, b)


def _alibi_slopes():
    n = N_DIFF_HEADS + N_DIL_HEADS
    return jnp.asarray(np.exp2(-ALIBI_MAX_EXP * np.arange(1, n + 1) / n), _F32)


def _layer(x, xb, layer_idx, wt_in, b_fz, lam_vecs, subln_g, w_out, ln1_g, ln1_b,
           ln2_g, ln2_b, slopes, dil_tables, batch, casts_todo, mlp_weights):
    m, d = x.shape
    s = m // batch
    if xb is None:
        ct, w_out_b, xb = _forget_cumsum(x, batch, wt_in, b_fz, w_out, layer_idx)
    else:
        ct, w_out_b = _forget_cumsum(xb, batch, wt_in, b_fz, w_out, layer_idx)
    ct = ct.reshape(batch, N_FOX_HEADS, 1, s)
    h3 = _in_projection(xb, wt_in, layer_idx).reshape(batch, s, -1)

    def attention(kernel, *args, **kwargs):
        if not casts_todo:
            return _attention_call(kernel, h3, *args, **kwargs)
        o, w_b = _attention_call(kernel, h3, *args, cast=casts_todo.pop(0), **kwargs)
        mlp_weights.append(w_b)
        return o

    o_fox = attention(
        _fox_kernel,
        [pl.BlockSpec((1, HEADS_PER_BLOCK, 1, s), lambda b, g: (b, g, 0, 0))],
        (ct,), (_SEG["fq"], _SEG["fk"], _SEG["fv"]))

    o_sb = attention(_sb_kernel, [], (), (_SEG["sq"], _SEG["sk"], _SEG["sv"]))

    lam_init = 0.8 - 0.6 * math.exp(-0.3 * layer_idx)
    o_diff = attention(
        functools.partial(_diff_kernel, lam_init=lam_init),
        [pl.BlockSpec((4, HEAD_DIM), lambda b, g, *_: (0, 0)),
         pl.BlockSpec((1, 2 * HEAD_DIM), lambda b, g, *_: (0, 0))],
        (lam_vecs, subln_g), (_SEG["dq"], _SEG["dk"], _SEG["dv"]), scalars=(slopes,))

    o_dil = attention(
        _dil_kernel,
        [pl.BlockSpec((ATT_T, s), lambda b, g, *_: (0, 0))] * 2,
        dil_tables, (_SEG["gq"], _SEG["gk"], _SEG["gv"]), scalars=(slopes,),
        scratch_shapes=[pltpu.VMEM((HEADS_PER_BLOCK, ATT_T, s), _F32)])

    mixed = [o.reshape(m, GROUP_WIDTH) for o in (o_fox, o_sb, o_diff, o_dil)]
    x, xb = _out_projection(mixed, w_out_b, x, ln1_g, ln1_b)
    w1, w2 = mlp_weights[2 * layer_idx:2 * layer_idx + 2]
    return _mlp(xb, x, w1, w2, ln2_g, ln2_b)


def kernel(x, w_in, fox_forget_bias, diff_lambda_q1, diff_lambda_k1, diff_lambda_q2, diff_lambda_k2,
           diff_subln_gain, w_out, ln1_gain, ln1_bias, w_mlp_in, w_mlp_out, ln2_gain, ln2_bias):
    batch, s, d = x.shape
    assert d == D_MODEL and s % ATT_T == 0
    slopes = _alibi_slopes()
    dil_tables = _dilated_tables(s)
    xf = x.reshape(batch * s, d)
    xb = None
    wt_in = jnp.transpose(w_in, (0, 2, 1))
    casts_todo = [(w, l) for l in range(DEPTH) for w in (w_mlp_in, w_mlp_out)]
    mlp_weights = []
    for l in range(DEPTH):
        b_fz = fox_forget_bias[l].reshape(N_FOX_HEADS, 1)
        lam_vecs = jnp.stack([diff_lambda_q1[l], diff_lambda_k1[l], diff_lambda_q2[l], diff_lambda_k2[l]])
        xf, xb = _layer(
            xf, xb, l, wt_in, b_fz, lam_vecs, diff_subln_gain[l].reshape(1, -1),
            w_out, ln1_gain[l].reshape(1, d), ln1_bias[l].reshape(1, d),
            ln2_gain[l].reshape(1, d), ln2_bias[l].reshape(1, d), slopes, dil_tables, batch,
            casts_todo, mlp_weights)
    return xf.reshape(batch, s, d)
```

```python
import functools
import math

import numpy as np
import jax
import jax.numpy as jnp
from jax import lax
from jax.experimental import pallas as pl
from jax.experimental.pallas import tpu as pltpu

D_MODEL = 2048
DEPTH = 2
HEAD_DIM = 64
GROUP_WIDTH = 512
N_FOX_HEADS = 8
N_DIFF_HEADS = 4
N_DIL_HEADS = 8
DILATED_BRANCHES = ((128, 1), (512, 4), (2048, 16))
ALIBI_MAX_EXP = 8.0
LN_EPS = 1e-5
RMS_EPS = 1e-5
DEEPNORM_ALPHA = (2 * DEPTH) ** 0.25
QK_SCALE = HEAD_DIM ** -0.5

LANES = 128
HEADS_PER_BLOCK = LANES // HEAD_DIM
VMEM_LIMIT_BYTES = 56 * 1024 * 1024
MASKED = -1e30

_SEG = {name: (GROUP_WIDTH // LANES) * i for i, name in enumerate(
    ("fq", "fk", "fv", "sq", "sk", "sv", "dq", "dk", "dv", "gq", "gk", "gv"))}

ATT_T = 256

_F32 = jnp.float32
_BF16 = jnp.bfloat16


def _params(*sem):
    return pltpu.CompilerParams(dimension_semantics=sem, vmem_limit_bytes=VMEM_LIMIT_BYTES)


def _dot(a, b):
    return jnp.dot(a, b, preferred_element_type=_F32)


def _dot_nt(a, b):
    return lax.dot_general(a, b, (((1,), (1,)), ((), ())), preferred_element_type=_F32)


FORGET_ROW0 = 3 * GROUP_WIDTH


def _in_proj_kernel(x_ref, wt_ref, o_ref, wb_ref):
    @pl.when(pl.program_id(1) == 0)
    def _():
        wb_ref[...] = wt_ref[0].astype(_BF16)

    o_ref[...] = _dot_nt(x_ref[...], wb_ref[...]).astype(o_ref.dtype)


def _in_projection(xb, wt, layer, tm=1024, tn=1536):
    m, k = xb.shape
    n = wt.shape[1] - N_FOX_HEADS
    assert FORGET_ROW0 % tn == 0

    def w_rows(j, i):
        groups = j * (tn // N_FOX_HEADS) + jnp.where(j * tn >= FORGET_ROW0, 1, 0)
        return (layer, groups * N_FOX_HEADS, 0)

    return pl.pallas_call(
        _in_proj_kernel,
        grid=(n // tn, m // tm),
        in_specs=[pl.BlockSpec((tm, k), lambda j, i: (i, 0)),
                  pl.BlockSpec((pl.Element(1), pl.Element(tn), pl.Element(k)), w_rows)],
        out_specs=pl.BlockSpec((tm, tn), lambda j, i: (i, j)),
        out_shape=jax.ShapeDtypeStruct((m, n), _BF16),
        scratch_shapes=[pltpu.VMEM((tn, k), _BF16)],
        compiler_params=_params("arbitrary", "arbitrary"),
    )(xb, wt)


def _split3(x):
    hi = x.astype(_BF16)
    r = x - hi.astype(_F32)
    mid = r.astype(_BF16)
    lo = (r - mid.astype(_F32)).astype(_BF16)
    return hi, mid, lo


def _log_sigmoid(z):
    return jnp.minimum(z, 0.0) - jnp.log(1.0 + jnp.exp(-jnp.abs(z)))


def _forget_kernel(x_ref, w_ref, b_ref, wo_ref, ct_ref, wob_ref, *rest, blk):
    *maybe_xb_ref, carry_ref = rest
    wob_ref[...] = wo_ref[...].astype(wob_ref.dtype)
    xb = x_ref[...].astype(_BF16)
    for xb_ref in maybe_xb_ref:
        xb_ref[...] = xb

    @pl.when(pl.program_id(1) == 0)
    def _():
        carry_ref[...] = jnp.zeros_like(carry_ref)

    z = _dot_nt(w_ref[...].astype(_BF16), xb) + b_ref[...]
    log_f = _log_sigmoid(z)
    r = lax.broadcasted_iota(jnp.int32, (blk, blk), 0)
    c = lax.broadcasted_iota(jnp.int32, (blk, blk), 1)
    upto = (r <= c).astype(_BF16)
    carry = carry_ref[...]
    for i in range(x_ref.shape[0] // blk):
        hi, mid, lo = _split3(log_f[:, i * blk:(i + 1) * blk])
        cum = _dot(hi, upto) + _dot(mid, upto) + _dot(lo, upto) + carry
        ct_ref[0, :, i * blk:(i + 1) * blk] = cum
        carry = cum[:, blk - 1:blk]
    carry_ref[...] = carry


def _forget_cumsum(x, b, wt, b_fz, w_out, layer, pieces=2, blk=256):
    m, d = x.shape
    rows = m // (b * pieces)
    slab = w_out.shape[1] // (b * pieces)
    flat = lambda i, j: (i * pieces + j, 0)
    out_specs = [pl.BlockSpec((1, N_FOX_HEADS, rows), lambda i, j: (i, 0, j)),
                 pl.BlockSpec((slab, w_out.shape[2]), flat)]
    out_shape = [jax.ShapeDtypeStruct((b, N_FOX_HEADS, m // b), _F32),
                 jax.ShapeDtypeStruct(w_out.shape[1:], _BF16)]
    if x.dtype != _BF16:
        out_specs.append(pl.BlockSpec((rows, d), flat))
        out_shape.append(jax.ShapeDtypeStruct((m, d), _BF16))
    return pl.pallas_call(
        functools.partial(_forget_kernel, blk=blk),
        grid=(b, pieces),
        in_specs=[pl.BlockSpec((rows, d), flat),
                  pl.BlockSpec((None, N_FOX_HEADS, d), lambda i, j: (layer, FORGET_ROW0 // N_FOX_HEADS, 0)),
                  pl.BlockSpec((N_FOX_HEADS, 1), lambda i, j: (0, 0)),
                  pl.BlockSpec((None, slab, w_out.shape[2]), lambda i, j: (layer, i * pieces + j, 0))],
        out_specs=out_specs,
        out_shape=out_shape,
        scratch_shapes=[pltpu.VMEM((N_FOX_HEADS, 1), _F32)],
        compiler_params=_params("arbitrary", "arbitrary"),
    )(x, wt, b_fz, w_out)


def _triangle(t):
    r = lax.broadcasted_iota(jnp.int32, (t, t), 0)
    c = lax.broadcasted_iota(jnp.int32, (t, t), 1)
    return r, c


def _head_lanes(h):
    lane = lax.broadcasted_iota(jnp.int32, (1, LANES), 1)
    return (lane >= h * HEAD_DIM) & (lane < (h + 1) * HEAD_DIM)


def _one_head(q_pair, h):
    return jnp.where(_head_lanes(h), q_pair, jnp.zeros_like(q_pair))


def _masked_with_max(s, causal):
    t, n = s.shape
    if causal is not None:
        diag = jnp.where(causal, s[:, n - t:], MASKED)
        s = diag if n == t else jnp.concatenate([s[:, :n - t], diag], axis=-1)
    return s, jnp.max(s, axis=-1, keepdims=True)


def _values_with_ones(v_pair, h):
    return jnp.where(_head_lanes(h), v_pair, jnp.ones_like(v_pair))


def _normalise(o):
    return o / pltpu.roll(o, HEAD_DIM, 1)


def _pipelined(units, *stages):
    carried = [{} for _ in stages]
    for step in range(len(units) + len(stages) - 1):
        for k, stage in enumerate(stages):
            u = step - k
            if 0 <= u < len(units):
                args = (units[u], carried[k - 1].pop(u)) if k else (units[u],)
                carried[k][u] = stage(*args)


def _fox_kernel(q_ref, k_ref, v_ref, ct_ref, o_ref):
    t = ATT_T
    s_len = q_ref.shape[1]
    row, col = _triangle(t)
    causal = row >= col
    first = _head_lanes(0)
    values = [_values_with_ones(v_ref[0], h) for h in range(HEADS_PER_BLOCK)]

    def scores(unit):
        i, h = unit
        n = (i + 1) * t
        c_row = ct_ref[0, h, :, :n]
        key_bias = c_row[:, n - 1:n] - c_row
        q_pair = q_ref[0, i * t:n, :] * QK_SCALE
        return _masked_with_max(_dot_nt(_one_head(q_pair, h), k_ref[0, :n, :]) + key_bias, causal)

    outs = {}

    def finish(unit, logits):
        i, h = unit
        n = (i + 1) * t
        s, m = logits
        outs[h] = _normalise(_dot(jnp.exp(s - m).astype(_BF16), values[h][:n]))
        if h == HEADS_PER_BLOCK - 1:
            o_ref[0, i * t:n, :] = jnp.where(first, outs[0], outs[1]).astype(o_ref.dtype)

    _pipelined([(i, h) for i in range(s_len // t) for h in range(HEADS_PER_BLOCK)], scores, finish)


def _sb_kernel(q_ref, k_ref, v_ref, o_ref):
    t = ATT_T
    s_len = q_ref.shape[1]
    row, col = _triangle(t)
    strict = row > col
    from_ones = (row >= col).astype(_BF16)
    from_ones2 = jnp.concatenate([from_ones, from_ones], axis=0)
    first = _head_lanes(0)

    def scores(unit):
        i, h = unit
        q_pair = q_ref[0, i * t:(i + 1) * t, :] * QK_SCALE
        return _dot_nt(_one_head(q_pair, h), k_ref[0, :(i + 1) * t, :])

    outs = {}

    def cumulate(unit, z):
        i, h = unit
        nz = -z
        log_rest = jnp.minimum(nz, 0.0) - jnp.log(1.0 + jnp.exp(jnp.minimum(z, nz)))
        incl, sums = [], []
        for c in range(i + 1):
            lr = log_rest[:, c * t:(c + 1) * t]
            if c == i:
                lr = jnp.where(strict, lr, 0.0)
            hi = lr.astype(_BF16)
            lo = (lr - hi.astype(_F32)).astype(_BF16)
            incl.append(_dot(jnp.concatenate([hi, lo], axis=-1), from_ones2))
            sums.append(incl[c][:, :1])
        return z, incl, sums

    def finish(unit, carried):
        i, h = unit
        n = (i + 1) * t
        z, incl, sums = carried
        tail = jnp.zeros((t, 1), _F32)
        chunks = [None] * (i + 1)
        for c in range(i, -1, -1):
            a = jnp.exp(z[:, c * t:(c + 1) * t] + incl[c] + tail)
            if c == i:
                a = jnp.where(strict, a, 0.0)
            chunks[c] = a.astype(_BF16)
            tail = tail + sums[c]
        a_all = chunks[0] if i == 0 else jnp.concatenate(chunks, axis=-1)
        outs[h] = _dot(a_all, v_ref[0, :n, :])
        if h == HEADS_PER_BLOCK - 1:
            o_ref[0, i * t:n, :] = jnp.where(first, outs[0], outs[1]).astype(o_ref.dtype)

    _pipelined([(i, h) for i in range(s_len // t) for h in range(HEADS_PER_BLOCK)],
               scores, cumulate, finish)


def _diff_kernel(slopes_ref, q_ref, k_ref, v_ref, lam_ref, g_ref, o_ref, *, lam_init):
    t = ATT_T
    s_len = q_ref.shape[1]
    slope = slopes_ref[pl.program_id(1)]
    row, col = _triangle(t)
    causal = row >= col
    lq1, lk1, lq2, lk2 = (lam_ref[n:n + 1, :] for n in range(4))
    lam = (jnp.exp(jnp.sum(lq1 * lk1, axis=-1, keepdims=True))
           - jnp.exp(jnp.sum(lq2 * lk2, axis=-1, keepdims=True)) + lam_init)
    gain = g_ref[...] * (1.0 - lam_init)

    def scores(unit):
        i, h = unit
        n = (i + 1) * t
        q_pair = q_ref[0, i * t:n, :] * QK_SCALE
        key_bias = slope * (lax.broadcasted_iota(jnp.int32, (1, n), 1) - (n - 1)).astype(_F32)
        return _masked_with_max(_dot_nt(_one_head(q_pair, h), k_ref[0, :n, :]) + key_bias, causal)

    values = jnp.concatenate([v_ref[0], jnp.ones_like(v_ref[0])], axis=-1)
    maps = {}

    def finish(unit, logits):
        i, h = unit
        n = (i + 1) * t
        s, m = logits
        o = _dot(jnp.exp(s - m).astype(_BF16), values[:n])
        maps[h] = o[:, :LANES] / o[:, LANES:]
        if h == 1:
            od = maps[0] - lam * maps[1]
            od = od * lax.rsqrt(jnp.mean(od * od, axis=-1, keepdims=True) + RMS_EPS)
            o_ref[0, i * t:n, :] = (od * gain).astype(o_ref.dtype)

    _pipelined([(i, h) for i in reversed(range(s_len // t)) for h in range(2)], scores, finish)


def _dil_kernel(slopes_ref, q_ref, k_ref, v_ref, logm_ref, ndist_ref, o_ref, bias_ref):
    t = ATT_T
    s_len = q_ref.shape[1]
    first = _head_lanes(0)
    values = [_values_with_ones(v_ref[0], h) for h in range(HEADS_PER_BLOCK)]
    for h in range(HEADS_PER_BLOCK):
        slope = slopes_ref[N_DIFF_HEADS + HEADS_PER_BLOCK * pl.program_id(1) + h]
        bias_ref[h] = logm_ref[...] + slope * ndist_ref[...]

    def scores(unit):
        i, h = unit
        n = (i + 1) * t
        q_pair = q_ref[0, i * t:n, :] * QK_SCALE
        s = _dot_nt(_one_head(q_pair, h), k_ref[0, :n, :]) + bias_ref[h, :, s_len - n:]
        return _masked_with_max(s, None)

    outs = {}

    def finish(unit, logits):
        i, h = unit
        n = (i + 1) * t
        s, m = logits
        outs[h] = _normalise(_dot(jnp.exp(s - m).astype(_BF16), values[h][:n]))
        if h == HEADS_PER_BLOCK - 1:
            o_ref[0, i * t:n, :] = jnp.where(first, outs[0], outs[1]).astype(o_ref.dtype)

    _pipelined([(i, h) for i in range(s_len // t) for h in range(HEADS_PER_BLOCK)], scores, finish)


def _also_casting(kernel, n_lead):
    def body(*refs):
        src, o_ref, dst = refs[n_lead:n_lead + 3]
        dst[...] = src[...].astype(dst.dtype)
        kernel(*refs[:n_lead], o_ref, *refs[n_lead + 3:])
    return body


def _attention_call(kernel, h3, extra_specs, extra_args, segs, scalars=(), scratch_shapes=(), cast=None):
    b, s, _ = h3.shape
    seg_q, seg_k, seg_v = segs
    n_groups = GROUP_WIDTH // LANES

    def col_block(seg):
        return pl.BlockSpec((1, s, LANES), lambda bb, g, *_: (bb, 0, seg + g))

    in_specs = [col_block(seg_q), col_block(seg_k), col_block(seg_v)] + list(extra_specs)
    out_specs = col_block(0)
    out_shape = jax.ShapeDtypeStruct((b, s, GROUP_WIDTH), _BF16)
    args = (*scalars, h3, h3, h3, *extra_args)
    if cast is not None:
        w, layer = cast
        _, rows, cols = w.shape
        slab = rows // (b * n_groups)
        kernel = _also_casting(kernel, len(args))
        in_specs.append(pl.BlockSpec((None, slab, cols), lambda bb, g, *_: (layer, bb * n_groups + g, 0)))
        out_specs = [out_specs, pl.BlockSpec((slab, cols), lambda bb, g, *_: (bb * n_groups + g, 0))]
        out_shape = [out_shape, jax.ShapeDtypeStruct((rows, cols), _BF16)]
        args = (*args, w)
    grid_spec = pltpu.PrefetchScalarGridSpec(
        num_scalar_prefetch=len(scalars),
        grid=(b, n_groups),
        in_specs=in_specs,
        out_specs=out_specs,
        scratch_shapes=list(scratch_shapes))
    return pl.pallas_call(
        kernel,
        grid_spec=grid_spec,
        out_shape=out_shape,
        compiler_params=_params("arbitrary", "arbitrary"),
    )(*args)


def _dilated_tables(s):
    r = lax.broadcasted_iota(jnp.int32, (ATT_T, s), 0)
    c = lax.broadcasted_iota(jnp.int32, (ATT_T, s), 1)
    dist = (s - ATT_T) + r - c
    mult = jnp.zeros(dist.shape, _F32)
    for window, dilation in DILATED_BRANCHES:
        hit = (dist >= 0) & (dist <= window) & (dist % dilation == 0)
        mult = mult + hit.astype(_F32)
    logm = jnp.where(mult > 0, jnp.log(jnp.maximum(mult, 1.0)), MASKED)
    return logm, -jnp.maximum(dist, 0).astype(_F32)


def _layer_norm(z, g, b):
    mu = jnp.mean(z, axis=-1, keepdims=True)
    zc = z - mu
    var = jnp.mean(zc * zc, axis=-1, keepdims=True)
    return zc * lax.rsqrt(var + LN_EPS) * g + b


def _out_proj_kernel(m0_ref, m1_ref, m2_ref, m3_ref, w_ref, x_ref, g_ref, b_ref, xo_ref, xob_ref):
    sub = OUT_PROJ_SUB_ROWS
    blocks = [slice(r, r + sub) for r in range(0, x_ref.shape[0], sub)]

    def project(rows):
        mixed = jnp.concatenate([m_ref[rows, :] for m_ref in (m0_ref, m1_ref, m2_ref, m3_ref)], axis=-1)
        return _dot(mixed, w_ref[...])

    def finish(rows, y):
        out = _layer_norm(DEEPNORM_ALPHA * x_ref[rows, :] + y, g_ref[...], b_ref[...])
        xo_ref[rows, :] = out
        xob_ref[rows, :] = out.astype(_BF16)

    _pipelined(blocks, project, finish)


OUT_PROJ_SUB_ROWS = 128


def _out_projection(mixed, w, x, g, b, tm=512):
    m, d = x.shape
    row = lambda i: (i, 0)
    const = lambda i: (0, 0)
    return pl.pallas_call(
        _out_proj_kernel,
        grid=(m // tm,),
        in_specs=[pl.BlockSpec((tm, GROUP_WIDTH), row)] * 4 + [
            pl.BlockSpec((d, d), const), pl.BlockSpec((tm, d), row),
            pl.BlockSpec((1, d), const), pl.BlockSpec((1, d), const)],
        out_specs=[pl.BlockSpec((tm, d), row), pl.BlockSpec((tm, d), row)],
        out_shape=[jax.ShapeDtypeStruct((m, d), _F32), jax.ShapeDtypeStruct((m, d), _BF16)],
        compiler_params=_params("arbitrary"),
    )(*mixed, w, x, g, b)


MLP_LN_SUB_ROWS = 128


def _mlp_kernel(xb_ref, x_ref, w1_hbm, w2_hbm, g_ref, b_ref, xo_ref, xob_ref,
                w1_buf, w2_buf, acc_ref, sems, *, tf):
    n_chunks = w1_hbm.shape[1] // tf
    assert n_chunks % 2 == 0 and n_chunks >= 2
    step, n_steps = pl.program_id(0), pl.num_programs(0)

    def w1_copy(c):
        return pltpu.make_async_copy(w1_hbm.at[:, pl.ds(c * tf, tf)], w1_buf.at[c % 2], sems.at[0, c % 2])

    def w2_copy(c):
        return pltpu.make_async_copy(w2_hbm.at[pl.ds(c * tf, tf), :], w2_buf.at[c % 2], sems.at[1, c % 2])

    def start(copy_of, c):
        if c < n_chunks:
            copy_of(c).start()
        else:
            pl.when(step + 1 < n_steps)(lambda: copy_of(c - n_chunks).start())

    @pl.when(step == 0)
    def _():
        w1_copy(0).start()
        w2_copy(0).start()
        w1_copy(1).start()

    w1_copy(0).wait()
    hidden = _dot(xb_ref[...], w1_buf[0])
    for c in range(n_chunks):
        start(w1_copy, c + 2)
        start(w2_copy, c + 1)
        if c + 1 < n_chunks:
            w1_copy(c + 1).wait()
        w2_copy(c).wait()
        a = jnp.maximum(hidden, 0.0)
        a = (a * a).astype(_BF16)
        if c + 1 < n_chunks:
            ahead = _dot(xb_ref[...], w1_buf[(c + 1) % 2])
            part = _dot(a, w2_buf[c % 2])
            if c == 0:
                acc_ref[...] = part
            else:
                acc_ref[...] += part
            hidden = ahead

    def last_part(rows):
        return _dot(a[rows, :], w2_buf[(n_chunks - 1) % 2])

    def finish(rows, part):
        y = acc_ref[rows, :] + part
        out = _layer_norm(DEEPNORM_ALPHA * x_ref[rows, :] + y, g_ref[...], b_ref[...])
        xo_ref[rows, :] = out
        xob_ref[rows, :] = out.astype(_BF16)

    _pipelined([slice(r, r + MLP_LN_SUB_ROWS) for r in range(0, x_ref.shape[0], MLP_LN_SUB_ROWS)],
               last_part, finish)


def _mlp(xb, x, w1, w2, g, b, tm=512, tf=1024):
    m, d = x.shape
    row = lambda i: (i, 0)
    const = lambda i: (0, 0)
    hbm = pl.BlockSpec(memory_space=pl.ANY)
    return pl.pallas_call(
        functools.partial(_mlp_kernel, tf=tf),
        grid=(m // tm,),
        in_specs=[pl.BlockSpec((tm, d), row), pl.BlockSpec((tm, d), row), hbm, hbm,
                  pl.BlockSpec((1, d), const), pl.BlockSpec((1, d), const)],
        out_specs=[pl.BlockSpec((tm, d), row), pl.BlockSpec((tm, d), row)],
        out_shape=[jax.ShapeDtypeStruct((m, d), _F32), jax.ShapeDtypeStruct((m, d), _BF16)],
        scratch_shapes=[pltpu.VMEM((2, d, tf), _BF16), pltpu.VMEM((2, tf, d), _BF16),
                        pltpu.VMEM((tm, d), _F32), pltpu.SemaphoreType.DMA((2, 2))],
        compiler_params=_params("arbitrary"),
    )(xb, x, w1, w2, g, b)


def _alibi_slopes():
    n = N_DIFF_HEADS + N_DIL_HEADS
    return jnp.asarray(np.exp2(-ALIBI_MAX_EXP * np.arange(1, n + 1) / n), _F32)


def _layer(x, xb, layer_idx, wt_in, b_fz, lam_vecs, subln_g, w_out, ln1_g, ln1_b,
           ln2_g, ln2_b, slopes, dil_tables, batch, casts_todo, mlp_weights):
    m, d = x.shape
    s = m // batch
    if xb is None:
        ct, w_out_b, xb = _forget_cumsum(x, batch, wt_in, b_fz, w_out, layer_idx)
    else:
        ct, w_out_b = _forget_cumsum(xb, batch, wt_in, b_fz, w_out, layer_idx)
    ct = ct.reshape(batch, N_FOX_HEADS, 1, s)
    h3 = _in_projection(xb, wt_in, layer_idx).reshape(batch, s, -1)

    def attention(kernel, *args, **kwargs):
        if not casts_todo:
            return _attention_call(kernel, h3, *args, **kwargs)
        o, w_b = _attention_call(kernel, h3, *args, cast=casts_todo.pop(0), **kwargs)
        mlp_weights.append(w_b)
        return o

    o_fox = attention(
        _fox_kernel,
        [pl.BlockSpec((1, HEADS_PER_BLOCK, 1, s), lambda b, g: (b, g, 0, 0))],
        (ct,), (_SEG["fq"], _SEG["fk"], _SEG["fv"]))

    o_sb = attention(_sb_kernel, [], (), (_SEG["sq"], _SEG["sk"], _SEG["sv"]))

    lam_init = 0.8 - 0.6 * math.exp(-0.3 * layer_idx)
    o_diff = attention(
        functools.partial(_diff_kernel, lam_init=lam_init),
        [pl.BlockSpec((4, HEAD_DIM), lambda b, g, *_: (0, 0)),
         pl.BlockSpec((1, 2 * HEAD_DIM), lambda b, g, *_: (0, 0))],
        (lam_vecs, subln_g), (_SEG["dq"], _SEG["dk"], _SEG["dv"]), scalars=(slopes,))

    o_dil = attention(
        _dil_kernel,
        [pl.BlockSpec((ATT_T, s), lambda b, g, *_: (0, 0))] * 2,
        dil_tables, (_SEG["gq"], _SEG["gk"], _SEG["gv"]), scalars=(slopes,),
        scratch_shapes=[pltpu.VMEM((HEADS_PER_BLOCK, ATT_T, s), _F32)])

    mixed = [o.reshape(m, GROUP_WIDTH) for o in (o_fox, o_sb, o_diff, o_dil)]
    x, xb = _out_projection(mixed, w_out_b, x, ln1_g, ln1_b)
    w1, w2 = mlp_weights[2 * layer_idx:2 * layer_idx + 2]
    return _mlp(xb, x, w1, w2, ln2_g, ln2_b)


def kernel(x, w_in, fox_forget_bias, diff_lambda_q1, diff_lambda_k1, diff_lambda_q2, diff_lambda_k2,
           diff_subln_gain, w_out, ln1_gain, ln1_bias, w_mlp_in, w_mlp_out, ln2_gain, ln2_bias):
    batch, s, d = x.shape
    assert d == D_MODEL and s % ATT_T == 0
    slopes = _alibi_slopes()
    dil_tables = _dilated_tables(s)
    xf = x.reshape(batch * s, d)
    xb = None
    wt_in = jnp.transpose(w_in, (0, 2, 1))
    casts_todo = [(w, l) for l in range(DEPTH) for w in (w_mlp_in, w_mlp_out)]
    mlp_weights = []
    for l in range(DEPTH):
        b_fz = fox_forget_bias[l].reshape(N_FOX_HEADS, 1)
        lam_vecs = jnp.stack([diff_lambda_q1[l], diff_lambda_k1[l], diff_lambda_q2[l], diff_lambda_k2[l]])
        xf, xb = _layer(
            xf, xb, l, wt_in, b_fz, lam_vecs, diff_subln_gain[l].reshape(1, -1),
            w_out, ln1_gain[l].reshape(1, d), ln1_bias[l].reshape(1, d),
            ln2_gain[l].reshape(1, d), ln2_bias[l].reshape(1, d), slopes, dil_tables, batch,
            casts_todo, mlp_weights)
    return xf.reshape(batch, s, d)
```

```python
import functools
import math

import numpy as np
import jax
import jax.numpy as jnp
from jax import lax
from jax.experimental import pallas as pl
from jax.experimental.pallas import tpu as pltpu

D_MODEL = 2048
DEPTH = 2
HEAD_DIM = 64
GROUP_WIDTH = 512
N_FOX_HEADS = 8
N_DIFF_HEADS = 4
N_DIL_HEADS = 8
DILATED_BRANCHES = ((128, 1), (512, 4), (2048, 16))
ALIBI_MAX_EXP = 8.0
LN_EPS = 1e-5
RMS_EPS = 1e-5
DEEPNORM_ALPHA = (2 * DEPTH) ** 0.25
QK_SCALE = HEAD_DIM ** -0.5

LANES = 128
HEADS_PER_BLOCK = LANES // HEAD_DIM
VMEM_LIMIT_BYTES = 56 * 1024 * 1024
MASKED = -1e30

_SEG = {name: (GROUP_WIDTH // LANES) * i for i, name in enumerate(
    ("fq", "fk", "fv", "sq", "sk", "sv", "dq", "dk", "dv", "gq", "gk", "gv"))}

ATT_T = 256

_F32 = jnp.float32
_BF16 = jnp.bfloat16


def _params(*sem):
    return pltpu.CompilerParams(dimension_semantics=sem, vmem_limit_bytes=VMEM_LIMIT_BYTES)


def _dot(a, b):
    return jnp.dot(a, b, preferred_element_type=_F32)


def _dot_nt(a, b):
    return lax.dot_general(a, b, (((1,), (1,)), ((), ())), preferred_element_type=_F32)


FORGET_ROW0 = 3 * GROUP_WIDTH


def _in_proj_kernel(x_ref, wt_ref, o_ref, wb_ref):
    @pl.when(pl.program_id(1) == 0)
    def _():
        wb_ref[...] = wt_ref[0].astype(_BF16)

    o_ref[...] = _dot_nt(x_ref[...], wb_ref[...]).astype(o_ref.dtype)


def _in_projection(xb, wt, layer, tm=1024, tn=1536):
    m, k = xb.shape
    n = wt.shape[1] - N_FOX_HEADS
    assert FORGET_ROW0 % tn == 0

    def w_rows(j, i):
        groups = j * (tn // N_FOX_HEADS) + jnp.where(j * tn >= FORGET_ROW0, 1, 0)
        return (layer, groups * N_FOX_HEADS, 0)

    return pl.pallas_call(
        _in_proj_kernel,
        grid=(n // tn, m // tm),
        in_specs=[pl.BlockSpec((tm, k), lambda j, i: (i, 0)),
                  pl.BlockSpec((pl.Element(1), pl.Element(tn), pl.Element(k)), w_rows)],
        out_specs=pl.BlockSpec((tm, tn), lambda j, i: (i, j)),
        out_shape=jax.ShapeDtypeStruct((m, n), _BF16),
        scratch_shapes=[pltpu.VMEM((tn, k), _BF16)],
        compiler_params=_params("arbitrary", "arbitrary"),
    )(xb, wt)


def _split3(x):
    hi = x.astype(_BF16)
    r = x - hi.astype(_F32)
    mid = r.astype(_BF16)
    lo = (r - mid.astype(_F32)).astype(_BF16)
    return hi, mid, lo


def _log_sigmoid(z):
    return jnp.minimum(z, 0.0) - jnp.log(1.0 + jnp.exp(-jnp.abs(z)))


def _forget_kernel(x_ref, w_ref, b_ref, wo_ref, ct_ref, wob_ref, *rest, blk):
    *maybe_xb_ref, carry_ref = rest
    wob_ref[...] = wo_ref[...].astype(wob_ref.dtype)
    xb = x_ref[...].astype(_BF16)
    for xb_ref in maybe_xb_ref:
        xb_ref[...] = xb

    @pl.when(pl.program_id(1) == 0)
    def _():
        carry_ref[...] = jnp.zeros_like(carry_ref)

    z = _dot_nt(w_ref[...].astype(_BF16), xb) + b_ref[...]
    log_f = _log_sigmoid(z)
    r = lax.broadcasted_iota(jnp.int32, (blk, blk), 0)
    c = lax.broadcasted_iota(jnp.int32, (blk, blk), 1)
    upto = (r <= c).astype(_BF16)
    carry = carry_ref[...]
    for i in range(x_ref.shape[0] // blk):
        hi, mid, lo = _split3(log_f[:, i * blk:(i + 1) * blk])
        cum = _dot(hi, upto) + _dot(mid, upto) + _dot(lo, upto) + carry
        ct_ref[0, :, i * blk:(i + 1) * blk] = cum
        carry = cum[:, blk - 1:blk]
    carry_ref[...] = carry


def _forget_cumsum(x, b, wt, b_fz, w_out, layer, pieces=2, blk=256):
    m, d = x.shape
    rows = m // (b * pieces)
    slab = w_out.shape[1] // (b * pieces)
    flat = lambda i, j: (i * pieces + j, 0)
    out_specs = [pl.BlockSpec((1, N_FOX_HEADS, rows), lambda i, j: (i, 0, j)),
                 pl.BlockSpec((slab, w_out.shape[2]), flat)]
    out_shape = [jax.ShapeDtypeStruct((b, N_FOX_HEADS, m // b), _F32),
                 jax.ShapeDtypeStruct(w_out.shape[1:], _BF16)]
    if x.dtype != _BF16:
        out_specs.append(pl.BlockSpec((rows, d), flat))
        out_shape.append(jax.ShapeDtypeStruct((m, d), _BF16))
    return pl.pallas_call(
        functools.partial(_forget_kernel, blk=blk),
        grid=(b, pieces),
        in_specs=[pl.BlockSpec((rows, d), flat),
                  pl.BlockSpec((None, N_FOX_HEADS, d), lambda i, j: (layer, FORGET_ROW0 // N_FOX_HEADS, 0)),
                  pl.BlockSpec((N_FOX_HEADS, 1), lambda i, j: (0, 0)),
                  pl.BlockSpec((None, slab, w_out.shape[2]), lambda i, j: (layer, i * pieces + j, 0))],
        out_specs=out_specs,
        out_shape=out_shape,
        scratch_shapes=[pltpu.VMEM((N_FOX_HEADS, 1), _F32)],
        compiler_params=_params("arbitrary", "arbitrary"),
    )(x, wt, b_fz, w_out)


def _triangle(t):
    r = lax.broadcasted_iota(jnp.int32, (t, t), 0)
    c = lax.broadcasted_iota(jnp.int32, (t, t), 1)
    return r, c


def _head_lanes(h):
    lane = lax.broadcasted_iota(jnp.int32, (1, LANES), 1)
    return (lane >= h * HEAD_DIM) & (lane < (h + 1) * HEAD_DIM)


def _one_head(q_pair, h):
    return jnp.where(_head_lanes(h), q_pair, jnp.zeros_like(q_pair))


def _masked_with_max(s, causal):
    t, n = s.shape
    if causal is not None:
        diag = jnp.where(causal, s[:, n - t:], MASKED)
        s = diag if n == t else jnp.concatenate([s[:, :n - t], diag], axis=-1)
    return s, jnp.max(s, axis=-1, keepdims=True)


def _values_with_ones(v_pair, h):
    return jnp.where(_head_lanes(h), v_pair, jnp.ones_like(v_pair))


def _normalise(o):
    return o / pltpu.roll(o, HEAD_DIM, 1)


def _pipelined(units, *stages):
    carried = [{} for _ in stages]
    for step in range(len(units) + len(stages) - 1):
        for k, stage in enumerate(stages):
            u = step - k
            if 0 <= u < len(units):
                args = (units[u], carried[k - 1].pop(u)) if k else (units[u],)
                carried[k][u] = stage(*args)


def _fox_kernel(q_ref, k_ref, v_ref, ct_ref, o_ref):
    t = ATT_T
    s_len = q_ref.shape[1]
    row, col = _triangle(t)
    causal = row >= col
    first = _head_lanes(0)
    values = [_values_with_ones(v_ref[0], h) for h in range(HEADS_PER_BLOCK)]

    def scores(unit):
        i, h = unit
        n = (i + 1) * t
        c_row = ct_ref[0, h, :, :n]
        key_bias = c_row[:, n - 1:n] - c_row
        q_pair = q_ref[0, i * t:n, :] * QK_SCALE
        return _masked_with_max(_dot_nt(_one_head(q_pair, h), k_ref[0, :n, :]) + key_bias, causal)

    outs = {}

    def finish(unit, logits):
        i, h = unit
        n = (i + 1) * t
        s, m = logits
        outs[h] = _normalise(_dot(jnp.exp(s - m).astype(_BF16), values[h][:n]))
        if h == HEADS_PER_BLOCK - 1:
            o_ref[0, i * t:n, :] = jnp.where(first, outs[0], outs[1]).astype(o_ref.dtype)

    _pipelined([(i, h) for i in range(s_len // t) for h in range(HEADS_PER_BLOCK)], scores, finish)


def _sb_kernel(q_ref, k_ref, v_ref, o_ref):
    t = ATT_T
    s_len = q_ref.shape[1]
    row, col = _triangle(t)
    strict = row > col
    from_ones = (row >= col).astype(_BF16)
    from_ones2 = jnp.concatenate([from_ones, from_ones], axis=0)
    first = _head_lanes(0)

    def scores(unit):
        i, h = unit
        q_pair = q_ref[0, i * t:(i + 1) * t, :] * QK_SCALE
        return _dot_nt(_one_head(q_pair, h), k_ref[0, :(i + 1) * t, :])

    outs = {}

    def cumulate(unit, z):
        i, h = unit
        nz = -z
        log_rest = jnp.minimum(nz, 0.0) - jnp.log(1.0 + jnp.exp(jnp.minimum(z, nz)))
        incl, sums = [], []
        for c in range(i + 1):
            lr = log_rest[:, c * t:(c + 1) * t]
            if c == i:
                lr = jnp.where(strict, lr, 0.0)
            hi = lr.astype(_BF16)
            lo = (lr - hi.astype(_F32)).astype(_BF16)
            incl.append(_dot(jnp.concatenate([hi, lo], axis=-1), from_ones2))
            sums.append(incl[c][:, :1])
        return z, incl, sums

    def finish(unit, carried):
        i, h = unit
        n = (i + 1) * t
        z, incl, sums = carried
        tail = jnp.zeros((t, 1), _F32)
        chunks = [None] * (i + 1)
        for c in range(i, -1, -1):
            a = jnp.exp(z[:, c * t:(c + 1) * t] + incl[c] + tail)
            if c == i:
                a = jnp.where(strict, a, 0.0)
            chunks[c] = a.astype(_BF16)
            tail = tail + sums[c]
        a_all = chunks[0] if i == 0 else jnp.concatenate(chunks, axis=-1)
        outs[h] = _dot(a_all, v_ref[0, :n, :])
        if h == HEADS_PER_BLOCK - 1:
            o_ref[0, i * t:n, :] = jnp.where(first, outs[0], outs[1]).astype(o_ref.dtype)

    _pipelined([(i, h) for i in range(s_len // t) for h in range(HEADS_PER_BLOCK)],
               scores, cumulate, finish)


def _diff_kernel(slopes_ref, q_ref, k_ref, v_ref, lam_ref, g_ref, o_ref, *, lam_init):
    t = ATT_T
    s_len = q_ref.shape[1]
    slope = slopes_ref[pl.program_id(1)]
    row, col = _triangle(t)
    causal = row >= col
    lq1, lk1, lq2, lk2 = (lam_ref[n:n + 1, :] for n in range(4))
    lam = (jnp.exp(jnp.sum(lq1 * lk1, axis=-1, keepdims=True))
           - jnp.exp(jnp.sum(lq2 * lk2, axis=-1, keepdims=True)) + lam_init)
    gain = g_ref[...] * (1.0 - lam_init)

    def scores(unit):
        i, h = unit
        n = (i + 1) * t
        q_pair = q_ref[0, i * t:n, :] * QK_SCALE
        key_bias = slope * (lax.broadcasted_iota(jnp.int32, (1, n), 1) - (n - 1)).astype(_F32)
        return _masked_with_max(_dot_nt(_one_head(q_pair, h), k_ref[0, :n, :]) + key_bias, causal)

    values = jnp.concatenate([v_ref[0], jnp.ones_like(v_ref[0])], axis=-1)
    maps = {}

    def finish(unit, logits):
        i, h = unit
        n = (i + 1) * t
        s, m = logits
        o = _dot(jnp.exp(s - m).astype(_BF16), values[:n])
        maps[h] = o[:, :LANES] / o[:, LANES:]
        if h == 1:
            od = maps[0] - lam * maps[1]
            od = od * lax.rsqrt(jnp.mean(od * od, axis=-1, keepdims=True) + RMS_EPS)
            o_ref[0, i * t:n, :] = (od * gain).astype(o_ref.dtype)

    _pipelined([(i, h) for i in reversed(range(s_len // t)) for h in range(2)], scores, finish)


def _dil_kernel(slopes_ref, q_ref, k_ref, v_ref, logm_ref, ndist_ref, o_ref, bias_ref):
    t = ATT_T
    s_len = q_ref.shape[1]
    first = _head_lanes(0)
    values = [_values_with_ones(v_ref[0], h) for h in range(HEADS_PER_BLOCK)]
    for h in range(HEADS_PER_BLOCK):
        slope = slopes_ref[N_DIFF_HEADS + HEADS_PER_BLOCK * pl.program_id(1) + h]
        bias_ref[h] = logm_ref[...] + slope * ndist_ref[...]

    def scores(unit):
        i, h = unit
        n = (i + 1) * t
        q_pair = q_ref[0, i * t:n, :] * QK_SCALE
        s = _dot_nt(_one_head(q_pair, h), k_ref[0, :n, :]) + bias_ref[h, :, s_len - n:]
        return _masked_with_max(s, None)

    outs = {}

    def finish(unit, logits):
        i, h = unit
        n = (i + 1) * t
        s, m = logits
        outs[h] = _normalise(_dot(jnp.exp(s - m).astype(_BF16), values[h][:n]))
        if h == HEADS_PER_BLOCK - 1:
            o_ref[0, i * t:n, :] = jnp.where(first, outs[0], outs[1]).astype(o_ref.dtype)

    _pipelined([(i, h) for i in range(s_len // t) for h in range(HEADS_PER_BLOCK)], scores, finish)


def _also_casting(kernel, n_lead):
    def body(*refs):
        src, o_ref, dst = refs[n_lead:n_lead + 3]
        dst[...] = src[...].astype(dst.dtype)
        kernel(*refs[:n_lead], o_ref, *refs[n_lead + 3:])
    return body


def _attention_call(kernel, h3, extra_specs, extra_args, segs, scalars=(), scratch_shapes=(), cast=None):
    b, s, _ = h3.shape
    seg_q, seg_k, seg_v = segs
    n_groups = GROUP_WIDTH // LANES

    def col_block(seg):
        return pl.BlockSpec((1, s, LANES), lambda bb, g, *_: (bb, 0, seg + g))

    in_specs = [col_block(seg_q), col_block(seg_k), col_block(seg_v)] + list(extra_specs)
    out_specs = col_block(0)
    out_shape = jax.ShapeDtypeStruct((b, s, GROUP_WIDTH), _BF16)
    args = (*scalars, h3, h3, h3, *extra_args)
    if cast is not None:
        w, layer = cast
        _, rows, cols = w.shape
        slab = rows // (b * n_groups)
        kernel = _also_casting(kernel, len(args))
        in_specs.append(pl.BlockSpec((None, slab, cols), lambda bb, g, *_: (layer, bb * n_groups + g, 0)))
        out_specs = [out_specs, pl.BlockSpec((slab, cols), lambda bb, g, *_: (bb * n_groups + g, 0))]
        out_shape = [out_shape, jax.ShapeDtypeStruct((rows, cols), _BF16)]
        args = (*args, w)
    grid_spec = pltpu.PrefetchScalarGridSpec(
        num_scalar_prefetch=len(scalars),
        grid=(b, n_groups),
        in_specs=in_specs,
        out_specs=out_specs,
        scratch_shapes=list(scratch_shapes))
    return pl.pallas_call(
        kernel,
        grid_spec=grid_spec,
        out_shape=out_shape,
        compiler_params=_params("arbitrary", "arbitrary"),
    )(*args)


def _dilated_tables(s):
    r = lax.broadcasted_iota(jnp.int32, (ATT_T, s), 0)
    c = lax.broadcasted_iota(jnp.int32, (ATT_T, s), 1)
    dist = (s - ATT_T) + r - c
    mult = jnp.zeros(dist.shape, _F32)
    for window, dilation in DILATED_BRANCHES:
        hit = (dist >= 0) & (dist <= window) & (dist % dilation == 0)
        mult = mult + hit.astype(_F32)
    logm = jnp.where(mult > 0, jnp.log(jnp.maximum(mult, 1.0)), MASKED)
    return logm, -jnp.maximum(dist, 0).astype(_F32)


def _layer_norm(z, g, b):
    mu = jnp.mean(z, axis=-1, keepdims=True)
    zc = z - mu
    var = jnp.mean(zc * zc, axis=-1, keepdims=True)
    return zc * lax.rsqrt(var + LN_EPS) * g + b


def _out_proj_kernel(m0_ref, m1_ref, m2_ref, m3_ref, w_ref, x_ref, g_ref, b_ref, xo_ref, xob_ref):
    sub = OUT_PROJ_SUB_ROWS
    blocks = [slice(r, r + sub) for r in range(0, x_ref.shape[0], sub)]

    def project(rows):
        mixed = jnp.concatenate([m_ref[rows, :] for m_ref in (m0_ref, m1_ref, m2_ref, m3_ref)], axis=-1)
        return _dot(mixed, w_ref[...])

    def finish(rows, y):
        out = _layer_norm(DEEPNORM_ALPHA * x_ref[rows, :] + y, g_ref[...], b_ref[...])
        xo_ref[rows, :] = out
        xob_ref[rows, :] = out.astype(_BF16)

    _pipelined(blocks, project, finish)


OUT_PROJ_SUB_ROWS = 128


def _out_projection(mixed, w, x, g, b, tm=512):
    m, d = x.shape
    row = lambda i: (i, 0)
    const = lambda i: (0, 0)
    return pl.pallas_call(
        _out_proj_kernel,
        grid=(m // tm,),
        in_specs=[pl.BlockSpec((tm, GROUP_WIDTH), row)] * 4 + [
            pl.BlockSpec((d, d), const), pl.BlockSpec((tm, d), row),
            pl.BlockSpec((1, d), const), pl.BlockSpec((1, d), const)],
        out_specs=[pl.BlockSpec((tm, d), row), pl.BlockSpec((tm, d), row)],
        out_shape=[jax.ShapeDtypeStruct((m, d), _F32), jax.ShapeDtypeStruct((m, d), _BF16)],
        compiler_params=_params("arbitrary"),
    )(*mixed, w, x, g, b)


MLP_LN_SUB_ROWS = 128
MLP_W2_DMA_PRIORITY = 1


def _mlp_kernel(xb_ref, x_ref, w1_hbm, w2_hbm, g_ref, b_ref, xo_ref, xob_ref,
                w1_buf, w2_buf, acc_ref, sems, *, tf):
    n_chunks = w1_hbm.shape[1] // tf
    assert n_chunks % 2 == 0 and n_chunks >= 2
    step, n_steps = pl.program_id(0), pl.num_programs(0)

    def w1_copy(c):
        return pltpu.make_async_copy(w1_hbm.at[:, pl.ds(c * tf, tf)], w1_buf.at[c % 2], sems.at[0, c % 2])

    def w2_copy(c):
        return pltpu.make_async_copy(w2_hbm.at[pl.ds(c * tf, tf), :], w2_buf.at[c % 2], sems.at[1, c % 2])

    def start(copy_of, c):
        priority = MLP_W2_DMA_PRIORITY if copy_of is w2_copy else 0
        if c < n_chunks:
            copy_of(c).start(priority=priority)
        else:
            pl.when(step + 1 < n_steps)(lambda: copy_of(c - n_chunks).start(priority=priority))

    @pl.when(step == 0)
    def _():
        start(w1_copy, 0)
        start(w2_copy, 0)
        start(w1_copy, 1)

    w1_copy(0).wait()
    hidden = _dot(xb_ref[...], w1_buf[0])
    for c in range(n_chunks):
        start(w1_copy, c + 2)
        start(w2_copy, c + 1)
        if c + 1 < n_chunks:
            w1_copy(c + 1).wait()
        w2_copy(c).wait()
        a = jnp.maximum(hidden, 0.0)
        a = (a * a).astype(_BF16)
        if c + 1 < n_chunks:
            ahead = _dot(xb_ref[...], w1_buf[(c + 1) % 2])
            part = _dot(a, w2_buf[c % 2])
            if c == 0:
                acc_ref[...] = part
            else:
                acc_ref[...] += part
            hidden = ahead

    def last_part(rows):
        return _dot(a[rows, :], w2_buf[(n_chunks - 1) % 2])

    def finish(rows, part):
        y = acc_ref[rows, :] + part
        out = _layer_norm(DEEPNORM_ALPHA * x_ref[rows, :] + y, g_ref[...], b_ref[...])
        xo_ref[rows, :] = out
        xob_ref[rows, :] = out.astype(_BF16)

    _pipelined([slice(r, r + MLP_LN_SUB_ROWS) for r in range(0, x_ref.shape[0], MLP_LN_SUB_ROWS)],
               last_part, finish)


def _mlp(xb, x, w1, w2, g, b, tm=512, tf=1024):
    m, d = x.shape
    row = lambda i: (i, 0)
    const = lambda i: (0, 0)
    hbm = pl.BlockSpec(memory_space=pl.ANY)
    return pl.pallas_call(
        functools.partial(_mlp_kernel, tf=tf),
        grid=(m // tm,),
        in_specs=[pl.BlockSpec((tm, d), row), pl.BlockSpec((tm, d), row), hbm, hbm,
                  pl.BlockSpec((1, d), const), pl.BlockSpec((1, d), const)],
        out_specs=[pl.BlockSpec((tm, d), row), pl.BlockSpec((tm, d), row)],
        out_shape=[jax.ShapeDtypeStruct((m, d), _F32), jax.ShapeDtypeStruct((m, d), _BF16)],
        scratch_shapes=[pltpu.VMEM((2, d, tf), _BF16), pltpu.VMEM((2, tf, d), _BF16),
                        pltpu.VMEM((tm, d), _F32), pltpu.SemaphoreType.DMA((2, 2))],
        compiler_params=_params("arbitrary"),
    )(xb, x, w1, w2, g, b)


def _alibi_slopes():
    n = N_DIFF_HEADS + N_DIL_HEADS
    return jnp.asarray(np.exp2(-ALIBI_MAX_EXP * np.arange(1, n + 1) / n), _F32)


def _layer(x, xb, layer_idx, wt_in, b_fz, lam_vecs, subln_g, w_out, ln1_g, ln1_b,
           ln2_g, ln2_b, slopes, dil_tables, batch, casts_todo, mlp_weights):
    m, d = x.shape
    s = m // batch
    if xb is None:
        ct, w_out_b, xb = _forget_cumsum(x, batch, wt_in, b_fz, w_out, layer_idx)
    else:
        ct, w_out_b = _forget_cumsum(xb, batch, wt_in, b_fz, w_out, layer_idx)
    ct = ct.reshape(batch, N_FOX_HEADS, 1, s)
    h3 = _in_projection(xb, wt_in, layer_idx).reshape(batch, s, -1)

    def attention(kernel, *args, **kwargs):
        if not casts_todo:
            return _attention_call(kernel, h3, *args, **kwargs)
        o, w_b = _attention_call(kernel, h3, *args, cast=casts_todo.pop(0), **kwargs)
        mlp_weights.append(w_b)
        return o

    o_fox = attention(
        _fox_kernel,
        [pl.BlockSpec((1, HEADS_PER_BLOCK, 1, s), lambda b, g: (b, g, 0, 0))],
        (ct,), (_SEG["fq"], _SEG["fk"], _SEG["fv"]))

    o_sb = attention(_sb_kernel, [], (), (_SEG["sq"], _SEG["sk"], _SEG["sv"]))

    lam_init = 0.8 - 0.6 * math.exp(-0.3 * layer_idx)
    o_diff = attention(
        functools.partial(_diff_kernel, lam_init=lam_init),
        [pl.BlockSpec((4, HEAD_DIM), lambda b, g, *_: (0, 0)),
         pl.BlockSpec((1, 2 * HEAD_DIM), lambda b, g, *_: (0, 0))],
        (lam_vecs, subln_g), (_SEG["dq"], _SEG["dk"], _SEG["dv"]), scalars=(slopes,))

    o_dil = attention(
        _dil_kernel,
        [pl.BlockSpec((ATT_T, s), lambda b, g, *_: (0, 0))] * 2,
        dil_tables, (_SEG["gq"], _SEG["gk"], _SEG["gv"]), scalars=(slopes,),
        scratch_shapes=[pltpu.VMEM((HEADS_PER_BLOCK, ATT_T, s), _F32)])

    mixed = [o.reshape(m, GROUP_WIDTH) for o in (o_fox, o_sb, o_diff, o_dil)]
    x, xb = _out_projection(mixed, w_out_b, x, ln1_g, ln1_b)
    w1, w2 = mlp_weights[2 * layer_idx:2 * layer_idx + 2]
    return _mlp(xb, x, w1, w2, ln2_g, ln2_b)


def kernel(x, w_in, fox_forget_bias, diff_lambda_q1, diff_lambda_k1, diff_lambda_q2, diff_lambda_k2,
           diff_subln_gain, w_out, ln1_gain, ln1_bias, w_mlp_in, w_mlp_out, ln2_gain, ln2_bias):
    batch, s, d = x.shape
    assert d == D_MODEL and s % ATT_T == 0
    slopes = _alibi_slopes()
    dil_tables = _dilated_tables(s)
    xf = x.reshape(batch * s, d)
    xb = None
    wt_in = jnp.transpose(w_in, (0, 2, 1))
    casts_todo = [(w, l) for l in range(DEPTH) for w in (w_mlp_in, w_mlp_out)]
    mlp_weights = []
    for l in range(DEPTH):
        b_fz = fox_forget_bias[l].reshape(N_FOX_HEADS, 1)
        lam_vecs = jnp.stack([diff_lambda_q1[l], diff_lambda_k1[l], diff_lambda_q2[l], diff_lambda_k2[l]])
        xf, xb = _layer(
            xf, xb, l, wt_in, b_fz, lam_vecs, diff_subln_gain[l].reshape(1, -1),
            w_out, ln1_gain[l].reshape(1, d), ln1_bias[l].reshape(1, d),
            ln2_gain[l].reshape(1, d), ln2_bias[l].reshape(1, d), slopes, dil_tables, batch,
            casts_todo, mlp_weights)
    return xf.reshape(batch, s, d)
```

```python
import functools
import math

import numpy as np
import jax
import jax.numpy as jnp
from jax import lax
from jax.experimental import pallas as pl
from jax.experimental.pallas import tpu as pltpu

D_MODEL = 2048
DEPTH = 2
HEAD_DIM = 64
GROUP_WIDTH = 512
N_FOX_HEADS = 8
N_DIFF_HEADS = 4
N_DIL_HEADS = 8
DILATED_BRANCHES = ((128, 1), (512, 4), (2048, 16))
ALIBI_MAX_EXP = 8.0
LN_EPS = 1e-5
RMS_EPS = 1e-5
DEEPNORM_ALPHA = (2 * DEPTH) ** 0.25
QK_SCALE = HEAD_DIM ** -0.5

LANES = 128
HEADS_PER_BLOCK = LANES // HEAD_DIM
VMEM_LIMIT_BYTES = 56 * 1024 * 1024
MASKED = -1e30

_SEG = {name: (GROUP_WIDTH // LANES) * i for i, name in enumerate(
    ("fq", "fk", "fv", "sq", "sk", "sv", "dq", "dk", "dv", "gq", "gk", "gv"))}

ATT_T = 256

_F32 = jnp.float32
_BF16 = jnp.bfloat16


def _params(*sem):
    return pltpu.CompilerParams(dimension_semantics=sem, vmem_limit_bytes=VMEM_LIMIT_BYTES)


def _dot(a, b):
    return jnp.dot(a, b, preferred_element_type=_F32)


def _dot_nt(a, b):
    return lax.dot_general(a, b, (((1,), (1,)), ((), ())), preferred_element_type=_F32)


FORGET_ROW0 = 3 * GROUP_WIDTH


def _in_proj_kernel(x_ref, wt_ref, o_ref, wb_ref):
    @pl.when(pl.program_id(1) == 0)
    def _():
        wb_ref[...] = wt_ref[0].astype(_BF16)

    o_ref[...] = _dot_nt(x_ref[...], wb_ref[...]).astype(o_ref.dtype)


def _in_projection(xb, wt, layer, tm=1024, tn=1536):
    m, k = xb.shape
    n = wt.shape[1] - N_FOX_HEADS
    assert FORGET_ROW0 % tn == 0

    def w_rows(j, i):
        groups = j * (tn // N_FOX_HEADS) + jnp.where(j * tn >= FORGET_ROW0, 1, 0)
        return (layer, groups * N_FOX_HEADS, 0)

    return pl.pallas_call(
        _in_proj_kernel,
        grid=(n // tn, m // tm),
        in_specs=[pl.BlockSpec((tm, k), lambda j, i: (i, 0)),
                  pl.BlockSpec((pl.Element(1), pl.Element(tn), pl.Element(k)), w_rows)],
        out_specs=pl.BlockSpec((tm, tn), lambda j, i: (i, j)),
        out_shape=jax.ShapeDtypeStruct((m, n), _BF16),
        scratch_shapes=[pltpu.VMEM((tn, k), _BF16)],
        compiler_params=_params("arbitrary", "arbitrary"),
    )(xb, wt)


def _split3(x):
    hi = x.astype(_BF16)
    r = x - hi.astype(_F32)
    mid = r.astype(_BF16)
    lo = (r - mid.astype(_F32)).astype(_BF16)
    return hi, mid, lo


def _log_sigmoid(z):
    return jnp.minimum(z, 0.0) - jnp.log(1.0 + jnp.exp(-jnp.abs(z)))


def _forget_kernel(x_ref, w_ref, b_ref, wo_ref, ct_ref, wob_ref, *rest, blk):
    *maybe_xb_ref, carry_ref = rest
    wob_ref[...] = wo_ref[...].astype(wob_ref.dtype)
    xb = x_ref[...].astype(_BF16)
    for xb_ref in maybe_xb_ref:
        xb_ref[...] = xb

    @pl.when(pl.program_id(1) == 0)
    def _():
        carry_ref[...] = jnp.zeros_like(carry_ref)

    z = _dot_nt(w_ref[...].astype(_BF16), xb) + b_ref[...]
    log_f = _log_sigmoid(z)
    r = lax.broadcasted_iota(jnp.int32, (blk, blk), 0)
    c = lax.broadcasted_iota(jnp.int32, (blk, blk), 1)
    upto = (r <= c).astype(_BF16)
    carry = carry_ref[...]
    for i in range(x_ref.shape[0] // blk):
        hi, mid, lo = _split3(log_f[:, i * blk:(i + 1) * blk])
        cum = _dot(hi, upto) + _dot(mid, upto) + _dot(lo, upto) + carry
        ct_ref[0, :, i * blk:(i + 1) * blk] = cum
        carry = cum[:, blk - 1:blk]
    carry_ref[...] = carry


def _forget_cumsum(x, b, wt, b_fz, w_out, layer, pieces=2, blk=256):
    m, d = x.shape
    rows = m // (b * pieces)
    slab = w_out.shape[1] // (b * pieces)
    flat = lambda i, j: (i * pieces + j, 0)
    out_specs = [pl.BlockSpec((1, N_FOX_HEADS, rows), lambda i, j: (i, 0, j)),
                 pl.BlockSpec((slab, w_out.shape[2]), flat)]
    out_shape = [jax.ShapeDtypeStruct((b, N_FOX_HEADS, m // b), _F32),
                 jax.ShapeDtypeStruct(w_out.shape[1:], _BF16)]
    if x.dtype != _BF16:
        out_specs.append(pl.BlockSpec((rows, d), flat))
        out_shape.append(jax.ShapeDtypeStruct((m, d), _BF16))
    return pl.pallas_call(
        functools.partial(_forget_kernel, blk=blk),
        grid=(b, pieces),
        in_specs=[pl.BlockSpec((rows, d), flat),
                  pl.BlockSpec((None, N_FOX_HEADS, d), lambda i, j: (layer, FORGET_ROW0 // N_FOX_HEADS, 0)),
                  pl.BlockSpec((N_FOX_HEADS, 1), lambda i, j: (0, 0)),
                  pl.BlockSpec((None, slab, w_out.shape[2]), lambda i, j: (layer, i * pieces + j, 0))],
        out_specs=out_specs,
        out_shape=out_shape,
        scratch_shapes=[pltpu.VMEM((N_FOX_HEADS, 1), _F32)],
        compiler_params=_params("arbitrary", "arbitrary"),
    )(x, wt, b_fz, w_out)


def _triangle(t):
    r = lax.broadcasted_iota(jnp.int32, (t, t), 0)
    c = lax.broadcasted_iota(jnp.int32, (t, t), 1)
    return r, c


def _head_lanes(h):
    lane = lax.broadcasted_iota(jnp.int32, (1, LANES), 1)
    return (lane >= h * HEAD_DIM) & (lane < (h + 1) * HEAD_DIM)


def _one_head(q_pair, h):
    return jnp.where(_head_lanes(h), q_pair, jnp.zeros_like(q_pair))


def _masked_with_max(s, causal):
    t, n = s.shape
    if causal is not None:
        diag = jnp.where(causal, s[:, n - t:], MASKED)
        s = diag if n == t else jnp.concatenate([s[:, :n - t], diag], axis=-1)
    return s, jnp.max(s, axis=-1, keepdims=True)


def _values_with_ones(v_pair, h):
    return jnp.where(_head_lanes(h), v_pair, jnp.ones_like(v_pair))


def _normalise(o):
    return o / pltpu.roll(o, HEAD_DIM, 1)


def _pipelined(units, *stages):
    carried = [{} for _ in stages]
    for step in range(len(units) + len(stages) - 1):
        for k, stage in enumerate(stages):
            u = step - k
            if 0 <= u < len(units):
                args = (units[u], carried[k - 1].pop(u)) if k else (units[u],)
                carried[k][u] = stage(*args)


def _fox_kernel(q_ref, k_ref, v_ref, ct_ref, o_ref):
    t = ATT_T
    s_len = q_ref.shape[1]
    row, col = _triangle(t)
    causal = row >= col
    first = _head_lanes(0)
    values = [_values_with_ones(v_ref[0], h) for h in range(HEADS_PER_BLOCK)]

    def scores(unit):
        i, h = unit
        n = (i + 1) * t
        c_row = ct_ref[0, h, :, :n]
        key_bias = c_row[:, n - 1:n] - c_row
        q_pair = q_ref[0, i * t:n, :] * QK_SCALE
        return _masked_with_max(_dot_nt(_one_head(q_pair, h), k_ref[0, :n, :]) + key_bias, causal)

    outs = {}

    def finish(unit, logits):
        i, h = unit
        n = (i + 1) * t
        s, m = logits
        outs[h] = _normalise(_dot(jnp.exp(s - m).astype(_BF16), values[h][:n]))
        if h == HEADS_PER_BLOCK - 1:
            o_ref[0, i * t:n, :] = jnp.where(first, outs[0], outs[1]).astype(o_ref.dtype)

    _pipelined([(i, h) for i in range(s_len // t) for h in range(HEADS_PER_BLOCK)], scores, finish)


def _sb_kernel(q_ref, k_ref, v_ref, o_ref):
    t = ATT_T
    s_len = q_ref.shape[1]
    row, col = _triangle(t)
    strict = row > col
    from_ones = (row >= col).astype(_BF16)
    from_ones2 = jnp.concatenate([from_ones, from_ones], axis=0)
    first = _head_lanes(0)

    def scores(unit):
        i, h = unit
        q_pair = q_ref[0, i * t:(i + 1) * t, :] * QK_SCALE
        return _dot_nt(_one_head(q_pair, h), k_ref[0, :(i + 1) * t, :])

    outs = {}

    def cumulate(unit, z):
        i, h = unit
        nz = -z
        log_rest = jnp.minimum(nz, 0.0) - jnp.log(1.0 + jnp.exp(jnp.minimum(z, nz)))
        incl, sums = [], []
        for c in range(i + 1):
            lr = log_rest[:, c * t:(c + 1) * t]
            if c == i:
                lr = jnp.where(strict, lr, 0.0)
            hi = lr.astype(_BF16)
            lo = (lr - hi.astype(_F32)).astype(_BF16)
            incl.append(_dot(jnp.concatenate([hi, lo], axis=-1), from_ones2))
            sums.append(incl[c][:, :1])
        return z, incl, sums

    def finish(unit, carried):
        i, h = unit
        n = (i + 1) * t
        z, incl, sums = carried
        tail = jnp.zeros((t, 1), _F32)
        chunks = [None] * (i + 1)
        for c in range(i, -1, -1):
            a = jnp.exp(z[:, c * t:(c + 1) * t] + incl[c] + tail)
            if c == i:
                a = jnp.where(strict, a, 0.0)
            chunks[c] = a.astype(_BF16)
            tail = tail + sums[c]
        a_all = chunks[0] if i == 0 else jnp.concatenate(chunks, axis=-1)
        outs[h] = _dot(a_all, v_ref[0, :n, :])
        if h == HEADS_PER_BLOCK - 1:
            o_ref[0, i * t:n, :] = jnp.where(first, outs[0], outs[1]).astype(o_ref.dtype)

    _pipelined([(i, h) for i in range(s_len // t) for h in range(HEADS_PER_BLOCK)],
               scores, cumulate, finish)


def _diff_kernel(slopes_ref, q_ref, k_ref, v_ref, lam_ref, g_ref, o_ref, *, lam_init):
    t = ATT_T
    s_len = q_ref.shape[1]
    slope = slopes_ref[pl.program_id(1)]
    row, col = _triangle(t)
    causal = row >= col
    lq1, lk1, lq2, lk2 = (lam_ref[n:n + 1, :] for n in range(4))
    lam = (jnp.exp(jnp.sum(lq1 * lk1, axis=-1, keepdims=True))
           - jnp.exp(jnp.sum(lq2 * lk2, axis=-1, keepdims=True)) + lam_init)
    gain = g_ref[...] * (1.0 - lam_init)

    def scores(unit):
        i, h = unit
        n = (i + 1) * t
        q_pair = q_ref[0, i * t:n, :] * QK_SCALE
        key_bias = slope * (lax.broadcasted_iota(jnp.int32, (1, n), 1) - (n - 1)).astype(_F32)
        return _masked_with_max(_dot_nt(_one_head(q_pair, h), k_ref[0, :n, :]) + key_bias, causal)

    values = jnp.concatenate([v_ref[0], jnp.ones_like(v_ref[0])], axis=-1)
    maps = {}

    def finish(unit, logits):
        i, h = unit
        n = (i + 1) * t
        s, m = logits
        o = _dot(jnp.exp(s - m).astype(_BF16), values[:n])
        maps[h] = o[:, :LANES] / o[:, LANES:]
        if h == 1:
            od = maps[0] - lam * maps[1]
            od = od * lax.rsqrt(jnp.mean(od * od, axis=-1, keepdims=True) + RMS_EPS)
            o_ref[0, i * t:n, :] = (od * gain).astype(o_ref.dtype)

    _pipelined([(i, h) for i in reversed(range(s_len // t)) for h in range(2)], scores, finish)


def _dil_kernel(slopes_ref, q_ref, k_ref, v_ref, logm_ref, ndist_ref, o_ref, bias_ref):
    t = ATT_T
    s_len = q_ref.shape[1]
    first = _head_lanes(0)
    values = [_values_with_ones(v_ref[0], h) for h in range(HEADS_PER_BLOCK)]
    for h in range(HEADS_PER_BLOCK):
        slope = slopes_ref[N_DIFF_HEADS + HEADS_PER_BLOCK * pl.program_id(1) + h]
        bias_ref[h] = logm_ref[...] + slope * ndist_ref[...]

    def scores(unit):
        i, h = unit
        n = (i + 1) * t
        q_pair = q_ref[0, i * t:n, :] * QK_SCALE
        s = _dot_nt(_one_head(q_pair, h), k_ref[0, :n, :]) + bias_ref[h, :, s_len - n:]
        return _masked_with_max(s, None)

    outs = {}

    def finish(unit, logits):
        i, h = unit
        n = (i + 1) * t
        s, m = logits
        outs[h] = _normalise(_dot(jnp.exp(s - m).astype(_BF16), values[h][:n]))
        if h == HEADS_PER_BLOCK - 1:
            o_ref[0, i * t:n, :] = jnp.where(first, outs[0], outs[1]).astype(o_ref.dtype)

    _pipelined([(i, h) for i in range(s_len // t) for h in range(HEADS_PER_BLOCK)], scores, finish)


def _also_casting(kernel, n_lead):
    def body(*refs):
        src, o_ref, dst = refs[n_lead:n_lead + 3]
        dst[...] = src[...].astype(dst.dtype)
        kernel(*refs[:n_lead], o_ref, *refs[n_lead + 3:])
    return body


def _attention_call(kernel, h3, extra_specs, extra_args, segs, scalars=(), scratch_shapes=(), cast=None):
    b, s, _ = h3.shape
    seg_q, seg_k, seg_v = segs
    n_groups = GROUP_WIDTH // LANES

    def col_block(seg):
        return pl.BlockSpec((1, s, LANES), lambda bb, g, *_: (bb, 0, seg + g))

    in_specs = [col_block(seg_q), col_block(seg_k), col_block(seg_v)] + list(extra_specs)
    out_specs = col_block(0)
    out_shape = jax.ShapeDtypeStruct((b, s, GROUP_WIDTH), _BF16)
    args = (*scalars, h3, h3, h3, *extra_args)
    if cast is not None:
        w, layer = cast
        _, rows, cols = w.shape
        slab = rows // (b * n_groups)
        kernel = _also_casting(kernel, len(args))
        in_specs.append(pl.BlockSpec((None, slab, cols), lambda bb, g, *_: (layer, bb * n_groups + g, 0)))
        out_specs = [out_specs, pl.BlockSpec((slab, cols), lambda bb, g, *_: (bb * n_groups + g, 0))]
        out_shape = [out_shape, jax.ShapeDtypeStruct((rows, cols), _BF16)]
        args = (*args, w)
    grid_spec = pltpu.PrefetchScalarGridSpec(
        num_scalar_prefetch=len(scalars),
        grid=(b, n_groups),
        in_specs=in_specs,
        out_specs=out_specs,
        scratch_shapes=list(scratch_shapes))
    return pl.pallas_call(
        kernel,
        grid_spec=grid_spec,
        out_shape=out_shape,
        compiler_params=_params("arbitrary", "arbitrary"),
    )(*args)


def _dilated_tables(s):
    r = lax.broadcasted_iota(jnp.int32, (ATT_T, s), 0)
    c = lax.broadcasted_iota(jnp.int32, (ATT_T, s), 1)
    dist = (s - ATT_T) + r - c
    mult = jnp.zeros(dist.shape, _F32)
    for window, dilation in DILATED_BRANCHES:
        hit = (dist >= 0) & (dist <= window) & (dist % dilation == 0)
        mult = mult + hit.astype(_F32)
    logm = jnp.where(mult > 0, jnp.log(jnp.maximum(mult, 1.0)), MASKED)
    return logm, -jnp.maximum(dist, 0).astype(_F32)


def _layer_norm(z, g, b):
    mu = jnp.mean(z, axis=-1, keepdims=True)
    zc = z - mu
    var = jnp.mean(zc * zc, axis=-1, keepdims=True)
    return zc * lax.rsqrt(var + LN_EPS) * g + b


def _out_proj_kernel(m0_ref, m1_ref, m2_ref, m3_ref, w_ref, x_ref, g_ref, b_ref, xo_ref, xob_ref):
    sub = OUT_PROJ_SUB_ROWS
    blocks = [slice(r, r + sub) for r in range(0, x_ref.shape[0], sub)]

    def project(rows):
        mixed = jnp.concatenate([m_ref[rows, :] for m_ref in (m0_ref, m1_ref, m2_ref, m3_ref)], axis=-1)
        return _dot(mixed, w_ref[...])

    def finish(rows, y):
        out = _layer_norm(DEEPNORM_ALPHA * x_ref[rows, :] + y, g_ref[...], b_ref[...])
        xo_ref[rows, :] = out
        xob_ref[rows, :] = out.astype(_BF16)

    _pipelined(blocks, project, finish)


OUT_PROJ_SUB_ROWS = 128


def _out_projection(mixed, w, x, g, b, tm=512):
    m, d = x.shape
    row = lambda i: (i, 0)
    const = lambda i: (0, 0)
    return pl.pallas_call(
        _out_proj_kernel,
        grid=(m // tm,),
        in_specs=[pl.BlockSpec((tm, GROUP_WIDTH), row)] * 4 + [
            pl.BlockSpec((d, d), const), pl.BlockSpec((tm, d), row),
            pl.BlockSpec((1, d), const), pl.BlockSpec((1, d), const)],
        out_specs=[pl.BlockSpec((tm, d), row), pl.BlockSpec((tm, d), row)],
        out_shape=[jax.ShapeDtypeStruct((m, d), _F32), jax.ShapeDtypeStruct((m, d), _BF16)],
        compiler_params=_params("arbitrary"),
    )(*mixed, w, x, g, b)


MLP_LN_SUB_ROWS = 128
MLP_WEIGHT_DMA_PRIORITY = 1


def _mlp_kernel(xb_ref, x_ref, w1_hbm, w2_hbm, g_ref, b_ref, xo_ref, xob_ref,
                w1_buf, w2_buf, acc_ref, sems, *, tf):
    n_chunks = w1_hbm.shape[1] // tf
    assert n_chunks % 2 == 0 and n_chunks >= 2
    step, n_steps = pl.program_id(0), pl.num_programs(0)

    def w1_copy(c):
        return pltpu.make_async_copy(w1_hbm.at[:, pl.ds(c * tf, tf)], w1_buf.at[c % 2], sems.at[0, c % 2])

    def w2_copy(c):
        return pltpu.make_async_copy(w2_hbm.at[pl.ds(c * tf, tf), :], w2_buf.at[c % 2], sems.at[1, c % 2])

    def start(copy_of, c):
        if c < n_chunks:
            copy_of(c).start(priority=MLP_WEIGHT_DMA_PRIORITY)
        else:
            pl.when(step + 1 < n_steps)(
                lambda: copy_of(c - n_chunks).start(priority=MLP_WEIGHT_DMA_PRIORITY))

    @pl.when(step == 0)
    def _():
        start(w1_copy, 0)
        start(w2_copy, 0)
        start(w1_copy, 1)

    w1_copy(0).wait()
    hidden = _dot(xb_ref[...], w1_buf[0])
    for c in range(n_chunks):
        start(w1_copy, c + 2)
        start(w2_copy, c + 1)
        if c + 1 < n_chunks:
            w1_copy(c + 1).wait()
        w2_copy(c).wait()
        a = jnp.maximum(hidden, 0.0)
        a = (a * a).astype(_BF16)
        if c + 1 < n_chunks:
            ahead = _dot(xb_ref[...], w1_buf[(c + 1) % 2])
            part = _dot(a, w2_buf[c % 2])
            if c == 0:
                acc_ref[...] = part
            else:
                acc_ref[...] += part
            hidden = ahead

    def last_part(rows):
        return _dot(a[rows, :], w2_buf[(n_chunks - 1) % 2])

    def finish(rows, part):
        y = acc_ref[rows, :] + part
        out = _layer_norm(DEEPNORM_ALPHA * x_ref[rows, :] + y, g_ref[...], b_ref[...])
        xo_ref[rows, :] = out
        xob_ref[rows, :] = out.astype(_BF16)

    _pipelined([slice(r, r + MLP_LN_SUB_ROWS) for r in range(0, x_ref.shape[0], MLP_LN_SUB_ROWS)],
               last_part, finish)


def _mlp(xb, x, w1, w2, g, b, tm=512, tf=1024):
    m, d = x.shape
    row = lambda i: (i, 0)
    const = lambda i: (0, 0)
    hbm = pl.BlockSpec(memory_space=pl.ANY)
    return pl.pallas_call(
        functools.partial(_mlp_kernel, tf=tf),
        grid=(m // tm,),
        in_specs=[pl.BlockSpec((tm, d), row), pl.BlockSpec((tm, d), row), hbm, hbm,
                  pl.BlockSpec((1, d), const), pl.BlockSpec((1, d), const)],
        out_specs=[pl.BlockSpec((tm, d), row), pl.BlockSpec((tm, d), row)],
        out_shape=[jax.ShapeDtypeStruct((m, d), _F32), jax.ShapeDtypeStruct((m, d), _BF16)],
        scratch_shapes=[pltpu.VMEM((2, d, tf), _BF16), pltpu.VMEM((2, tf, d), _BF16),
                        pltpu.VMEM((tm, d), _F32), pltpu.SemaphoreType.DMA((2, 2))],
        compiler_params=_params("arbitrary"),
    )(xb, x, w1, w2, g, b)


def _alibi_slopes():
    n = N_DIFF_HEADS + N_DIL_HEADS
    return jnp.asarray(np.exp2(-ALIBI_MAX_EXP * np.arange(1, n + 1) / n), _F32)


def _layer(x, xb, layer_idx, wt_in, b_fz, lam_vecs, subln_g, w_out, ln1_g, ln1_b,
           ln2_g, ln2_b, slopes, dil_tables, batch, casts_todo, mlp_weights):
    m, d = x.shape
    s = m // batch
    if xb is None:
        ct, w_out_b, xb = _forget_cumsum(x, batch, wt_in, b_fz, w_out, layer_idx)
    else:
        ct, w_out_b = _forget_cumsum(xb, batch, wt_in, b_fz, w_out, layer_idx)
    ct = ct.reshape(batch, N_FOX_HEADS, 1, s)
    h3 = _in_projection(xb, wt_in, layer_idx).reshape(batch, s, -1)

    def attention(kernel, *args, **kwargs):
        if not casts_todo:
            return _attention_call(kernel, h3, *args, **kwargs)
        o, w_b = _attention_call(kernel, h3, *args, cast=casts_todo.pop(0), **kwargs)
        mlp_weights.append(w_b)
        return o

    o_fox = attention(
        _fox_kernel,
        [pl.BlockSpec((1, HEADS_PER_BLOCK, 1, s), lambda b, g: (b, g, 0, 0))],
        (ct,), (_SEG["fq"], _SEG["fk"], _SEG["fv"]))

    o_sb = attention(_sb_kernel, [], (), (_SEG["sq"], _SEG["sk"], _SEG["sv"]))

    lam_init = 0.8 - 0.6 * math.exp(-0.3 * layer_idx)
    o_diff = attention(
        functools.partial(_diff_kernel, lam_init=lam_init),
        [pl.BlockSpec((4, HEAD_DIM), lambda b, g, *_: (0, 0)),
         pl.BlockSpec((1, 2 * HEAD_DIM), lambda b, g, *_: (0, 0))],
        (lam_vecs, subln_g), (_SEG["dq"], _SEG["dk"], _SEG["dv"]), scalars=(slopes,))

    o_dil = attention(
        _dil_kernel,
        [pl.BlockSpec((ATT_T, s), lambda b, g, *_: (0, 0))] * 2,
        dil_tables, (_SEG["gq"], _SEG["gk"], _SEG["gv"]), scalars=(slopes,),
        scratch_shapes=[pltpu.VMEM((HEADS_PER_BLOCK, ATT_T, s), _F32)])

    mixed = [o.reshape(m, GROUP_WIDTH) for o in (o_fox, o_sb, o_diff, o_dil)]
    x, xb = _out_projection(mixed, w_out_b, x, ln1_g, ln1_b)
    w1, w2 = mlp_weights[2 * layer_idx:2 * layer_idx + 2]
    return _mlp(xb, x, w1, w2, ln2_g, ln2_b)


def kernel(x, w_in, fox_forget_bias, diff_lambda_q1, diff_lambda_k1, diff_lambda_q2, diff_lambda_k2,
           diff_subln_gain, w_out, ln1_gain, ln1_bias, w_mlp_in, w_mlp_out, ln2_gain, ln2_bias):
    batch, s, d = x.shape
    assert d == D_MODEL and s % ATT_T == 0
    slopes = _alibi_slopes()
    dil_tables = _dilated_tables(s)
    xf = x.reshape(batch * s, d)
    xb = None
    wt_in = jnp.transpose(w_in, (0, 2, 1))
    casts_todo = [(w, l) for l in range(DEPTH) for w in (w_mlp_in, w_mlp_out)]
    mlp_weights = []
    for l in range(DEPTH):
        b_fz = fox_forget_bias[l].reshape(N_FOX_HEADS, 1)
        lam_vecs = jnp.stack([diff_lambda_q1[l], diff_lambda_k1[l], diff_lambda_q2[l], diff_lambda_k2[l]])
        xf, xb = _layer(
            xf, xb, l, wt_in, b_fz, lam_vecs, diff_subln_gain[l].reshape(1, -1),
            w_out, ln1_gain[l].reshape(1, d), ln1_bias[l].reshape(1, d),
            ln2_gain[l].reshape(1, d), ln2_bias[l].reshape(1, d), slopes, dil_tables, batch,
            casts_todo, mlp_weights)
    return xf.reshape(batch, s, d)
```
